```python
import jax, jax.numpy as jnp
from jax import lax
import numpy as np

D_MODEL = 1024
BATCH = 16
SEQ = 4096
DEPTH = 1
DEC_BATCH = 32
DEC_SEQ = 64
PAST_LEN = 2048

CHUNK = 64
GMLP_CHUNK = 128
GMLP_WIDTH = 1024
GMLP_GROUPS = 8
GMLP_GROUP_DIM = GMLP_WIDTH // GMLP_GROUPS
N_HEADS = 16
N_KV_HEADS = 4
GQA_GROUP = N_HEADS // N_KV_HEADS
HEAD_DIM = 64
ATTN_WIDTH = N_HEADS * HEAD_DIM
KV_WIDTH = N_KV_HEADS * HEAD_DIM
WINDOW = 128
WIN_CHUNKS = WINDOW // CHUNK
D_FF = 2816
CONV_W = 3
EPS = 1e-6
NEG_INF = -1e30
SPLITS = [GMLP_WIDTH, 2 * GMLP_WIDTH, 2 * GMLP_WIDTH + ATTN_WIDTH,
          2 * GMLP_WIDTH + ATTN_WIDTH + KV_WIDTH, 2 * GMLP_WIDTH + ATTN_WIDTH + 2 * KV_WIDTH]
IN_COLS = 2 * GMLP_WIDTH + ATTN_WIDTH + 2 * KV_WIDTH + 2 * D_MODEL

kernel_name = 'hybrid_gmlp_swa_convffn_stream_step'


def rms_norm(x, g):
    xf = x.astype(jnp.float32)
    y = xf * lax.rsqrt(jnp.mean(xf * xf, axis=-1, keepdims=True) + EPS)
    return (y * g.astype(jnp.float32)).astype(x.dtype)


def layer_norm(x, g, b):
    xf = x.astype(jnp.float32)
    mu = jnp.mean(xf, axis=-1, keepdims=True)
    xc = xf - mu
    y = xc * lax.rsqrt(jnp.mean(xc * xc, axis=-1, keepdims=True) + EPS)
    return (y * g.astype(jnp.float32) + b.astype(jnp.float32)).astype(x.dtype)


def alibi_slopes():
    return jnp.exp2(-8.0 * jnp.arange(1, N_HEADS + 1, dtype=jnp.float32) / N_HEADS)


def gmlp_spatial(u, vn, w_s, b_s):
    B, S, _ = u.shape
    L = min(S, GMLP_CHUNK)
    nc = S // L
    idx = jnp.arange(L)
    mask = (idx[None, :] // CHUNK) <= (idx[:, None] // CHUNK)
    w = w_s[:, :L, :L] * mask[None].astype(w_s.dtype)
    vg = vn.reshape(B, nc, L, GMLP_GROUPS, GMLP_GROUP_DIM)
    s = jnp.einsum('gts,bnsgc->bntgc', w, vg) + b_s[:, :L].T[None, None, :, :, None]
    return u * s.reshape(B, S, GMLP_WIDTH)


def sink_attend(qc, kb, vb, dist, valid, sinks):
    s = jnp.einsum('bnqhgd,bnshd->bnhgqs', qc, kb).astype(jnp.float32) * (HEAD_DIM ** -0.5)
    slopes = alibi_slopes().reshape(N_KV_HEADS, GQA_GROUP, 1, 1)
    s = jnp.where(valid, s - slopes * dist.astype(jnp.float32), NEG_INF)
    sink = sinks.astype(jnp.float32).reshape(N_KV_HEADS, GQA_GROUP, 1, 1)
    m = jnp.maximum(jnp.max(s, axis=-1, keepdims=True), sink)
    e = jnp.exp(s - m)
    p = e / (jnp.sum(e, axis=-1, keepdims=True) + jnp.exp(sink - m))
    return jnp.einsum('bnhgqs,bnshd->bnqhgd', p.astype(vb.dtype), vb)


def swa_prompt(q, k, v, sinks):
    B, S = q.shape[0], q.shape[1]
    nc = S // CHUNK
    nb = WIN_CHUNKS + 1
    pad = ((0, 0), (WINDOW, 0), (0, 0), (0, 0))
    kc = jnp.pad(k, pad).reshape(B, nc + WIN_CHUNKS, CHUNK, N_KV_HEADS, HEAD_DIM)
    vc = jnp.pad(v, pad).reshape(B, nc + WIN_CHUNKS, CHUNK, N_KV_HEADS, HEAD_DIM)
    kb = jnp.concatenate([kc[:, i:i + nc] for i in range(nb)], axis=2)
    vb = jnp.concatenate([vc[:, i:i + nc] for i in range(nb)], axis=2)
    qc = q.reshape(B, nc, CHUNK, N_KV_HEADS, GQA_GROUP, HEAD_DIM)
    qi = jnp.arange(CHUNK)
    kj = jnp.arange(nb * CHUNK)
    dist = jnp.abs(qi[:, None] + WINDOW - kj[None, :])
    kpos = jnp.arange(nc)[:, None] * CHUNK - WINDOW + kj[None, :]
    valid = (kpos >= 0)[None, :, None, None, None, :]
    o = sink_attend(qc, kb, vb, dist, valid, sinks)
    return o.reshape(B, S, ATTN_WIDTH)


def swa_sample(q, k, v, cache_k, cache_v, sinks):
    B, n = q.shape[0], q.shape[1]
    w = cache_k.shape[1]
    kb = jnp.concatenate([cache_k, k], axis=1)[:, None]
    vb = jnp.concatenate([cache_v, v], axis=1)[:, None]
    qc = q.reshape(B, 1, n, N_KV_HEADS, GQA_GROUP, HEAD_DIM)
    dist = jnp.abs(jnp.arange(n)[:, None] + w - jnp.arange(w + n)[None, :])
    valid = jnp.ones((w + n,), dtype=bool)
    o = sink_attend(qc, kb, vb, dist, valid, sinks)
    return o.reshape(B, n, ATTN_WIDTH)


def conv_ffn(h, w_up, w_conv, b_conv, w_down, conv_state):
    B, S, _ = h.shape
    up = h @ w_up
    a, val = jnp.split(up, 2, axis=-1)
    hist = jnp.zeros((B, CONV_W - 1, D_FF), a.dtype) if conv_state is None else conv_state
    a_ext = jnp.concatenate([hist, a], axis=1)
    c = b_conv + sum(w_conv[i] * a_ext[:, i:i + S] for i in range(CONV_W))
    out = (jax.nn.gelu(c) * val) @ w_down
    return out, a_ext[:, -(CONV_W - 1):]


def trunk(x, params, caches):
    (ln_mix_g, w_in, ln_v_g, ln_v_b, w_s, b_s, attn_sinks, w_pa, w_pb, w_o,
     ln_ffn_g, w_up, w_conv, b_conv, w_down, ln_final_g) = params
    B, S, _ = x.shape
    new_k, new_v, new_conv, new_gv = [], [], [], []
    for l in range(DEPTH):
        h = rms_norm(x, ln_mix_g[l])
        proj = h @ w_in[l]
        u, vg, q, k, v, gates = jnp.split(proj, SPLITS, axis=-1)
        vn = layer_norm(jax.nn.gelu(vg), ln_v_g[l], ln_v_b[l])
        a_out = gmlp_spatial(jax.nn.gelu(u), vn, w_s[l], b_s[l])
        q = q.reshape(B, S, N_HEADS, HEAD_DIM)
        k = k.reshape(B, S, N_KV_HEADS, HEAD_DIM)
        v = v.reshape(B, S, N_KV_HEADS, HEAD_DIM)
        if caches is None:
            b_out = swa_prompt(q, k, v, attn_sinks[l])
            keep = min(WINDOW, S)
            new_k.append(k[:, S - keep:])
            new_v.append(v[:, S - keep:])
            conv_state = None
        else:
            b_out = swa_sample(q, k, v, caches[0][l], caches[1][l], attn_sinks[l])
            new_k.append(k)
            new_v.append(v)
            new_gv.append(vn)
            conv_state = caches[2][l]
        gate_a, gate_b = jnp.split(jax.nn.sigmoid(gates), 2, axis=-1)
        merged = gate_a * (a_out @ w_pa[l]) + gate_b * (b_out @ w_pb[l])
        x = x + merged @ w_o[l]
        f, conv_new = conv_ffn(rms_norm(x, ln_ffn_g[l]), w_up[l], w_conv[l], b_conv[l], w_down[l], conv_state)
        new_conv.append(conv_new)
        x = x + f
    y = rms_norm(x, ln_final_g)
    gv = jnp.stack(new_gv) if caches is not None else None
    return y, jnp.stack(new_k), jnp.stack(new_v), jnp.stack(new_conv), gv


def setup_inputs(seed: int = 0) -> dict:
    key = jax.random.key(seed)
    ks = jax.random.split(key, 24)
    f32 = jnp.float32
    rows = min(WINDOW, PAST_LEN)
    nrm = lambda k, shape, s: jax.random.normal(k, shape, f32) * s
    return {
        'x_prompt': nrm(ks[0], (BATCH, SEQ, D_MODEL), 1.0),
        'x_sample': nrm(ks[1], (DEC_BATCH, DEC_SEQ, D_MODEL), 1.0),
        'cache_k': nrm(ks[2], (DEPTH, DEC_BATCH, rows, N_KV_HEADS, HEAD_DIM), 1.0),
        'cache_v': nrm(ks[3], (DEPTH, DEC_BATCH, rows, N_KV_HEADS, HEAD_DIM), 1.0),
        'cache_conv': nrm(ks[4], (DEPTH, DEC_BATCH, CONV_W - 1, D_FF), 1.0),
        'ln_mix_g': 1.0 + nrm(ks[5], (DEPTH, D_MODEL), 0.02),
        'w_in': nrm(ks[6], (DEPTH, D_MODEL, IN_COLS), D_MODEL ** -0.5),
        'ln_v_g': 1.0 + nrm(ks[7], (DEPTH, GMLP_WIDTH), 0.02),
        'ln_v_b': nrm(ks[8], (DEPTH, GMLP_WIDTH), 0.02),
        'w_s': nrm(ks[9], (DEPTH, GMLP_GROUPS, GMLP_CHUNK, GMLP_CHUNK), GMLP_CHUNK ** -0.5),
        'b_s': 1.0 + nrm(ks[10], (DEPTH, GMLP_GROUPS, GMLP_CHUNK), 0.1),
        'attn_sinks': nrm(ks[11], (DEPTH, N_HEADS), 0.5),
        'w_pa': nrm(ks[12], (DEPTH, GMLP_WIDTH, D_MODEL), GMLP_WIDTH ** -0.5),
        'w_pb': nrm(ks[13], (DEPTH, ATTN_WIDTH, D_MODEL), ATTN_WIDTH ** -0.5),
        'w_o': nrm(ks[14], (DEPTH, D_MODEL, D_MODEL), D_MODEL ** -0.5),
        'ln_ffn_g': 1.0 + nrm(ks[15], (DEPTH, D_MODEL), 0.02),
        'w_up': nrm(ks[16], (DEPTH, D_MODEL, 2 * D_FF), D_MODEL ** -0.5),
        'w_conv': nrm(ks[17], (DEPTH, CONV_W, D_FF), CONV_W ** -0.5),
        'b_conv': nrm(ks[18], (DEPTH, D_FF), 0.02),
        'w_down': nrm(ks[19], (DEPTH, D_FF, D_MODEL), D_FF ** -0.5),
        'ln_final_g': 1.0 + nrm(ks[20], (D_MODEL,), 0.02),
    }


def reference(x_prompt, x_sample, cache_k, cache_v, cache_conv, ln_mix_g, w_in, ln_v_g, ln_v_b,
              w_s, b_s, attn_sinks, w_pa, w_pb, w_o, ln_ffn_g, w_up, w_conv, b_conv, w_down, ln_final_g):
    params = (ln_mix_g, w_in, ln_v_g, ln_v_b, w_s, b_s, attn_sinks, w_pa, w_pb, w_o,
              ln_ffn_g, w_up, w_conv, b_conv, w_down, ln_final_g)
    y_prompt, new_k_prompt, new_v_prompt, new_conv_prompt, _ = trunk(x_prompt, params, None)
    y_sample, new_k_sample, new_v_sample, new_conv_sample, new_gmlp_v_sample = trunk(
        x_sample, params, (cache_k, cache_v, cache_conv))
    return (y_prompt, y_sample, new_k_prompt, new_v_prompt, new_conv_prompt,
            new_k_sample, new_v_sample, new_conv_sample, new_gmlp_v_sample)
```

```python
import functools

import jax
import jax.numpy as jnp
from jax import lax
from jax.experimental import pallas as pl
from jax.experimental.pallas import tpu as pltpu

F32 = jnp.float32
BF16 = jnp.bfloat16

D_MODEL = 1024
GMLP_WIDTH = 1024
GMLP_GROUPS = 8
GMLP_CHUNK = 128
N_HEADS = 16
N_KV_HEADS = 4
GQA_GROUP = N_HEADS // N_KV_HEADS
HEAD_DIM = 64
ATTN_WIDTH = N_HEADS * HEAD_DIM
KV_WIDTH = N_KV_HEADS * HEAD_DIM
CHUNK = 64
WINDOW = 128
KEYS = WINDOW + CHUNK
D_FF = 2816
CONV_W = 3
EPS = 1e-6
NEG_INF = -1e30
COL_U = 0
COL_VG = COL_U + GMLP_WIDTH
COL_Q = COL_VG + GMLP_WIDTH
COL_K = COL_Q + ATTN_WIDTH
COL_V = COL_K + KV_WIDTH
COL_GATE = COL_V + KV_WIDTH
IN_COLS = COL_GATE + 2 * D_MODEL

LANES = 128
SUBLANES = 8
HALF = LANES // 2

TILE = 512
FF_CHUNK = 256
MERGE_CHUNK = 512
VMEM_LIMIT_BYTES = 52 * 1024 * 1024


def _rms_norm(x, g):
    return x * lax.rsqrt(jnp.mean(x * x, axis=-1, keepdims=True) + EPS) * g


def _layer_norm(x, g, b):
    mu = jnp.mean(x, axis=-1, keepdims=True)
    xc = x - mu
    return xc * lax.rsqrt(jnp.mean(xc * xc, axis=-1, keepdims=True) + EPS) * g + b


def _col(x, c):
    return x[:, c * LANES:(c + 1) * LANES]


def _swap_halves(x):
    n = x.shape[1] // LANES
    return jnp.concatenate([pltpu.roll(_col(x, c), HALF, axis=1) for c in range(n)], axis=1)


def _low_half_mask():
    return lax.broadcasted_iota(jnp.int32, (1, LANES), 1) < HALF


def _project(x, lng_ref, win_ref, lvg_ref, lvb_ref, h_scr, u_scr, vn_scr, q_scr):
    h_scr[...] = _rms_norm(x, lng_ref[...]).astype(BF16)

    def proj(c0, n):
        return jnp.dot(h_scr[...], win_ref[:, c0:c0 + n], preferred_element_type=F32)

    u_scr[...] = jax.nn.gelu(proj(COL_U, GMLP_WIDTH))
    vn = _layer_norm(jax.nn.gelu(proj(COL_VG, GMLP_WIDTH)), lvg_ref[...], lvb_ref[...])
    vn_scr[...] = vn.astype(BF16)
    q = proj(COL_Q, ATTN_WIDTH) * (HEAD_DIM ** -0.5)
    qs = _swap_halves(q)
    blocks = []
    for c in range(ATTN_WIDTH // LANES):
        blocks += [_col(q, c), _col(qs, c)]
    q_scr[...] = jnp.concatenate(blocks, axis=1).astype(BF16)
    kv = proj(COL_K, 2 * KV_WIDTH)
    return kv[:, :KV_WIDTH], kv[:, KV_WIDTH:], vn


def _kv_blocks(k, v):
    lo = _low_half_mask()
    ks, vs = _swap_halves(k), _swap_halves(v)
    kb, vb = [], []
    for j in range(N_KV_HEADS):
        c = j // 2
        if j % 2 == 0:
            kb.append(jnp.where(lo, _col(k, c), 0.0))
            vb.append(jnp.where(lo, _col(v, c), _col(vs, c)))
        else:
            kb.append(jnp.where(lo, _col(ks, c), 0.0))
            vb.append(jnp.where(lo, _col(vs, c), _col(v, c)))
    return jnp.concatenate(kb, axis=1).astype(BF16), jnp.concatenate(vb, axis=1).astype(BF16)


def _attend_chunk(qc, kctx, vctx, sink_ref, bias_ref, first_valid_key):
    lo = _low_half_mask()
    ones = jnp.ones((KEYS, LANES), BF16)
    outs = []
    for j in range(N_KV_HEADS):
        heads = [GQA_GROUP * j + g for g in range(GQA_GROUP)]
        qs = jnp.concatenate([_col(qc, h) for h in heads], axis=0)
        s = lax.dot_general(qs, kctx(j), (((1,), (1,)), ((), ())),
                            preferred_element_type=F32)
        es, ms = [], []
        for g, h in enumerate(heads):
            sg = s[g * CHUNK:(g + 1) * CHUNK] - bias_ref[h]
            if first_valid_key is not None:
                kj = lax.broadcasted_iota(jnp.int32, (1, KEYS), 1)
                sg = jnp.where(kj >= first_valid_key, sg, NEG_INF)
            m = jnp.maximum(jnp.max(sg, axis=-1, keepdims=True), sink_ref[h])
            es.append(jnp.exp(sg - m).astype(BF16))
            ms.append(m)
        o = jnp.dot(jnp.concatenate(es, axis=0), jnp.concatenate([vctx(j), ones], axis=1),
                    preferred_element_type=F32)
        for p in range(GQA_GROUP // 2):
            g0, g1 = 2 * p, 2 * p + 1
            o0, o1 = o[g0 * CHUNK:(g0 + 1) * CHUNK], o[g1 * CHUNK:(g1 + 1) * CHUNK]
            num = jnp.where(lo, o0[:, :LANES], o1[:, :LANES])
            den = jnp.where(lo, o0[:, LANES:], o1[:, LANES:])
            sink_term = jnp.where(lo, jnp.exp(sink_ref[heads[g0]] - ms[g0]),
                                  jnp.exp(sink_ref[heads[g1]] - ms[g1]))
            outs.append(num / (den + sink_term))
    return outs


def _merge(x, h_scr, win_ref, a_scr, b_scr, m_scr, wpa_ref, wpb_ref, wo_ref):
    for c0 in range(0, D_MODEL, MERGE_CHUNK):
        c1 = c0 + MERGE_CHUNK
        gate_a = jax.nn.sigmoid(jnp.dot(h_scr[...], win_ref[:, COL_GATE + c0:COL_GATE + c1],
                                        preferred_element_type=F32))
        gate_b = jax.nn.sigmoid(jnp.dot(h_scr[...], win_ref[:, COL_GATE + D_MODEL + c0:COL_GATE + D_MODEL + c1],
                                        preferred_element_type=F32))
        ya = jnp.dot(a_scr[...], wpa_ref[:, c0:c1], preferred_element_type=F32)
        yb = jnp.dot(b_scr[...], wpb_ref[:, c0:c1], preferred_element_type=F32)
        m_scr[:, c0:c1] = (gate_a * ya + gate_b * yb).astype(BF16)
    return x + jnp.dot(m_scr[...], wo_ref[...], preferred_element_type=F32)


def _mixer_prompt_kernel(x_ref, lng_ref, win_ref, lvg_ref, lvb_ref, ws_ref, bsb_ref, sink_ref, bias_ref,
                         wpa_ref, wpb_ref, wo_ref,
                         x1_ref, knew_ref, vnew_ref,
                         h_scr, u_scr, vn_scr, q_scr, k_scr, v_scr, a_scr, b_scr, m_scr):
    tile = pl.program_id(1)
    x = x_ref[...]
    k, v, _ = _project(x, lng_ref, win_ref, lvg_ref, lvb_ref, h_scr, u_scr, vn_scr, q_scr)
    knew_ref[0] = k[TILE - WINDOW:]
    vnew_ref[0] = v[TILE - WINDOW:]

    ri = lax.broadcasted_iota(jnp.int32, (GMLP_CHUNK, GMLP_CHUNK), 0)
    ci = lax.broadcasted_iota(jnp.int32, (GMLP_CHUNK, GMLP_CHUNK), 1)
    causal = jnp.where((ci < CHUNK) | (ri >= CHUNK), 1.0, 0.0).astype(F32)
    n_blk = TILE // GMLP_CHUNK
    for g in range(GMLP_GROUPS):
        wm = (ws_ref[g] * causal).astype(BF16)
        vcat = jnp.concatenate(
            [vn_scr[n * GMLP_CHUNK:(n + 1) * GMLP_CHUNK, g * LANES:(g + 1) * LANES] for n in range(n_blk)], axis=1)
        sp = jnp.dot(wm, vcat, preferred_element_type=F32)
        for n in range(n_blk):
            rows = slice(n * GMLP_CHUNK, (n + 1) * GMLP_CHUNK)
            cols = slice(g * LANES, (g + 1) * LANES)
            a_scr[rows, cols] = (u_scr[rows, cols] * (_col(sp, n) + bsb_ref[g])).astype(BF16)

    @pl.when(tile == 0)
    def _():
        k_scr[0:WINDOW, :] = jnp.zeros((WINDOW, N_KV_HEADS * LANES), BF16)
        v_scr[0:WINDOW, :] = jnp.zeros((WINDOW, N_KV_HEADS * LANES), BF16)

    kb, vb = _kv_blocks(k, v)
    k_scr[WINDOW:, :] = kb
    v_scr[WINDOW:, :] = vb
    for c in range(TILE // CHUNK):
        r0 = c * CHUNK
        first_valid = None
        if c * CHUNK < WINDOW:
            first_valid = jnp.where(tile == 0, WINDOW - c * CHUNK, 0)
        outs = _attend_chunk(
            q_scr[r0:r0 + CHUNK, :],
            lambda j: k_scr[r0:r0 + KEYS, j * LANES:(j + 1) * LANES],
            lambda j: v_scr[r0:r0 + KEYS, j * LANES:(j + 1) * LANES],
            sink_ref, bias_ref, first_valid)
        b_scr[r0:r0 + CHUNK, :] = jnp.concatenate(outs, axis=1).astype(BF16)
    k_scr[0:WINDOW, :] = k_scr[TILE:TILE + WINDOW, :]
    v_scr[0:WINDOW, :] = v_scr[TILE:TILE + WINDOW, :]

    x1_ref[...] = _merge(x, h_scr, win_ref, a_scr, b_scr, m_scr, wpa_ref, wpb_ref, wo_ref)


def _mixer_sample_kernel(x_ref, lng_ref, win_ref, lvg_ref, lvb_ref, ws_ref, bsb_ref, sink_ref, bias_ref,
                         wpa_ref, wpb_ref, wo_ref, ck_ref, cv_ref,
                         x1_ref, knew_ref, vnew_ref, gv_ref,
                         h_scr, u_scr, vn_scr, q_scr, k_scr, v_scr, a_scr, b_scr, m_scr):
    n_seq = TILE // CHUNK
    x = x_ref[...]
    k, v, vn = _project(x, lng_ref, win_ref, lvg_ref, lvb_ref, h_scr, u_scr, vn_scr, q_scr)
    gv_ref[...] = vn
    for c in range(n_seq):
        knew_ref[c] = k[c * CHUNK:(c + 1) * CHUNK]
        vnew_ref[c] = v[c * CHUNK:(c + 1) * CHUNK]

    for g in range(GMLP_GROUPS):
        wm = ws_ref[g][:CHUNK, :CHUNK].astype(BF16)
        vcat = jnp.concatenate(
            [vn_scr[c * CHUNK:(c + 1) * CHUNK, g * LANES:(g + 1) * LANES] for c in range(n_seq)], axis=1)
        sp = jnp.dot(wm, vcat, preferred_element_type=F32)
        bs = bsb_ref[g][:CHUNK]
        for c in range(n_seq):
            rows = slice(c * CHUNK, (c + 1) * CHUNK)
            cols = slice(g * LANES, (g + 1) * LANES)
            a_scr[rows, cols] = (u_scr[rows, cols] * (_col(sp, c) + bs)).astype(BF16)

    kb, vb = _kv_blocks(k, v)
    for c in range(n_seq):
        ckb, cvb = _kv_blocks(ck_ref[c], cv_ref[c])
        k_scr[c, 0:WINDOW, :] = ckb
        v_scr[c, 0:WINDOW, :] = cvb
        k_scr[c, WINDOW:, :] = kb[c * CHUNK:(c + 1) * CHUNK]
        v_scr[c, WINDOW:, :] = vb[c * CHUNK:(c + 1) * CHUNK]
    for c in range(n_seq):
        r0 = c * CHUNK
        outs = _attend_chunk(
            q_scr[r0:r0 + CHUNK, :],
            lambda j: k_scr[c, :, j * LANES:(j + 1) * LANES],
            lambda j: v_scr[c, :, j * LANES:(j + 1) * LANES],
            sink_ref, bias_ref, None)
        b_scr[r0:r0 + CHUNK, :] = jnp.concatenate(outs, axis=1).astype(BF16)

    x1_ref[...] = _merge(x, h_scr, win_ref, a_scr, b_scr, m_scr, wpa_ref, wpb_ref, wo_ref)


def _ffn_kernel(*refs, sample):
    if sample:
        (x1_ref, lnf_ref, wup_ref, wconv_ref, bconv_ref, wdown_ref, lnfin_ref, cconv_ref,
         y_ref, cnew_ref, h2_scr, g_scr, a_buf) = refs
        seq_len = CHUNK
    else:
        (x1_ref, lnf_ref, wup_ref, wconv_ref, bconv_ref, wdown_ref, lnfin_ref,
         y_ref, cnew_ref, h2_scr, g_scr, a_buf, hist_scr) = refs
        seq_len = TILE

        @pl.when(pl.program_id(1) == 0)
        def _():
            hist_scr[...] = jnp.zeros(hist_scr.shape, F32)

    n_seq = TILE // seq_len
    x1 = x1_ref[...]
    h2_scr[...] = _rms_norm(x1, lnf_ref[...]).astype(BF16)
    for c0 in range(0, D_FF, FF_CHUNK):
        cols = slice(c0, c0 + FF_CHUNK)
        a = jnp.dot(h2_scr[...], wup_ref[:, cols], preferred_element_type=F32)
        val = jnp.dot(h2_scr[...], wup_ref[:, D_FF + c0:D_FF + c0 + FF_CHUNK], preferred_element_type=F32)
        w = wconv_ref[:, cols]
        taps = []
        for s in range(n_seq):
            a_s = a[s * seq_len:(s + 1) * seq_len]
            if sample:
                a_buf[s, SUBLANES - (CONV_W - 1):SUBLANES, :] = cconv_ref[s, :, cols]
            else:
                a_buf[s, 0:SUBLANES, :] = hist_scr[:, cols]
                hist_scr[:, cols] = a_s[seq_len - SUBLANES:]
            a_buf[s, SUBLANES:, :] = a_s
            cnew_ref[s, :, cols] = a_s[seq_len - (CONV_W - 1):]
            taps.append((w[0:1] * a_buf[s, SUBLANES - 2:SUBLANES - 2 + seq_len, :]
                         + w[1:2] * a_buf[s, SUBLANES - 1:SUBLANES - 1 + seq_len, :])
                        + w[2:3] * a_s)
        conv = bconv_ref[:, cols] + (taps[0] if n_seq == 1 else jnp.concatenate(taps, axis=0))
        g_scr[:, cols] = (jax.nn.gelu(conv) * val).astype(BF16)
    x2 = x1 + jnp.dot(g_scr[...], wdown_ref[...], preferred_element_type=F32)
    y_ref[...] = _rms_norm(x2, lnfin_ref[...])


def _resident(shape):
    nd = len(shape)
    return pl.BlockSpec(shape, lambda *_: (0,) * nd, pipeline_mode=pl.Buffered(1))


def _alibi_bias():
    slopes = jnp.exp2(-8.0 * jnp.arange(1, N_HEADS + 1, dtype=F32) / N_HEADS)
    qi = jnp.arange(CHUNK)[:, None]
    kj = jnp.arange(KEYS)[None, :]
    dist = jnp.abs(qi + WINDOW - kj).astype(F32)
    return slopes[:, None, None] * dist[None]


def _mixer(x2d, n_batch, params, caches):
    lng, win, lvg, lvb, ws, bsb, sinks, bias, wpa, wpb, wo = params
    tokens = x2d.shape[0]
    sample = caches is not None
    common_in = [
        None,
        _resident(lng.shape), _resident(win.shape), _resident(lvg.shape), _resident(lvb.shape),
        _resident(ws.shape), _resident(bsb.shape),
        pl.BlockSpec(memory_space=pltpu.SMEM),
        _resident(bias.shape), _resident(wpa.shape), _resident(wpb.shape), _resident(wo.shape),
    ]
    common_scratch = [
        pltpu.VMEM((TILE, D_MODEL), BF16),
        pltpu.VMEM((TILE, GMLP_WIDTH), F32),
        pltpu.VMEM((TILE, GMLP_WIDTH), BF16),
        pltpu.VMEM((TILE, N_HEADS * LANES), BF16),
    ]
    tail_scratch = [
        pltpu.VMEM((TILE, GMLP_WIDTH), BF16),
        pltpu.VMEM((TILE, ATTN_WIDTH), BF16),
        pltpu.VMEM((TILE, D_MODEL), BF16),
    ]
    if not sample:
        n_tiles = tokens // n_batch // TILE
        grid = (n_batch, n_tiles)
        tok = pl.BlockSpec((TILE, D_MODEL), lambda b, s: (b * n_tiles + s, 0))
        kv_out = pl.BlockSpec((1, WINDOW, KV_WIDTH), lambda b, s: (b, 0, 0))
        in_specs = [tok] + common_in[1:]
        out_specs = [tok, kv_out, kv_out]
        out_shape = [jax.ShapeDtypeStruct((tokens, D_MODEL), F32),
                     jax.ShapeDtypeStruct((n_batch, WINDOW, KV_WIDTH), F32),
                     jax.ShapeDtypeStruct((n_batch, WINDOW, KV_WIDTH), F32)]
        kv_scratch = [pltpu.VMEM((WINDOW + TILE, N_KV_HEADS * LANES), BF16)] * 2
        body, args, name = _mixer_prompt_kernel, (), "mixer_prompt"
    else:
        n_seq = TILE // CHUNK
        grid = (tokens // TILE,)
        tok = pl.BlockSpec((TILE, D_MODEL), lambda t: (t, 0))
        cache_spec = pl.BlockSpec((n_seq, WINDOW, KV_WIDTH), lambda t: (t, 0, 0))
        kv_out = pl.BlockSpec((n_seq, CHUNK, KV_WIDTH), lambda t: (t, 0, 0))
        in_specs = [tok] + common_in[1:] + [cache_spec, cache_spec]
        out_specs = [tok, kv_out, kv_out, pl.BlockSpec((TILE, GMLP_WIDTH), lambda t: (t, 0))]
        out_shape = [jax.ShapeDtypeStruct((tokens, D_MODEL), F32),
                     jax.ShapeDtypeStruct((n_batch, CHUNK, KV_WIDTH), F32),
                     jax.ShapeDtypeStruct((n_batch, CHUNK, KV_WIDTH), F32),
                     jax.ShapeDtypeStruct((tokens, GMLP_WIDTH), F32)]
        kv_scratch = [pltpu.VMEM((n_seq, KEYS, N_KV_HEADS * LANES), BF16)] * 2
        body, args, name = _mixer_sample_kernel, tuple(caches), "mixer_sample"
    return pl.pallas_call(
        body,
        grid=grid,
        in_specs=in_specs,
        out_specs=out_specs,
        out_shape=out_shape,
        scratch_shapes=common_scratch + kv_scratch + tail_scratch,
        compiler_params=pltpu.CompilerParams(
            dimension_semantics=("arbitrary",) * len(grid), vmem_limit_bytes=VMEM_LIMIT_BYTES),
        name=name,
    )(x2d, lng, win, lvg, lvb, ws, bsb, sinks, bias, wpa, wpb, wo, *args)


def _ffn(x2d, n_batch, params, cache_conv):
    lnf, wup, wconv, bconv, wdown, lnfin = params
    tokens = x2d.shape[0]
    sample = cache_conv is not None
    weights_in = [_resident(lnf.shape), _resident(wup.shape), _resident(wconv.shape), _resident(bconv.shape),
                  _resident(wdown.shape), _resident(lnfin.shape)]
    scratch = [pltpu.VMEM((TILE, D_MODEL), BF16),
               pltpu.VMEM((TILE, D_FF), BF16)]
    if not sample:
        n_tiles = tokens // n_batch // TILE
        grid = (n_batch, n_tiles)
        tok = pl.BlockSpec((TILE, D_MODEL), lambda b, s: (b * n_tiles + s, 0))
        in_specs = [tok] + weights_in
        conv_out = pl.BlockSpec((1, CONV_W - 1, D_FF), lambda b, s: (b, 0, 0))
        scratch += [pltpu.VMEM((1, SUBLANES + TILE, FF_CHUNK), F32),
                    pltpu.VMEM((SUBLANES, D_FF), F32)]
        args, name = (), "ffn_prompt"
    else:
        n_seq = TILE // CHUNK
        grid = (tokens // TILE,)
        tok = pl.BlockSpec((TILE, D_MODEL), lambda t: (t, 0))
        conv_out = pl.BlockSpec((n_seq, CONV_W - 1, D_FF), lambda t: (t, 0, 0))
        in_specs = [tok] + weights_in + [conv_out]
        scratch += [pltpu.VMEM((n_seq, SUBLANES + CHUNK, FF_CHUNK), F32)]
        args, name = (cache_conv,), "ffn_sample"
    return pl.pallas_call(
        functools.partial(_ffn_kernel, sample=sample),
        grid=grid,
        in_specs=in_specs,
        out_specs=[tok, conv_out],
        out_shape=[jax.ShapeDtypeStruct((tokens, D_MODEL), F32),
                   jax.ShapeDtypeStruct((n_batch, CONV_W - 1, D_FF), F32)],
        scratch_shapes=scratch,
        compiler_params=pltpu.CompilerParams(
            dimension_semantics=("arbitrary",) * len(grid), vmem_limit_bytes=VMEM_LIMIT_BYTES),
        name=name,
    )(x2d, lnf, wup, wconv, bconv, wdown, lnfin, *args)


def kernel(x_prompt, x_sample, cache_k, cache_v, cache_conv, ln_mix_g, w_in, ln_v_g, ln_v_b, w_s, b_s,
           attn_sinks, w_pa, w_pb, w_o, ln_ffn_g, w_up, w_conv, b_conv, w_down, ln_final_g):
    depth = w_in.shape[0]
    assert depth == 1, "single-layer kernel"
    n_prompt, seq, _ = x_prompt.shape
    n_sample, dec_seq, _ = x_sample.shape
    assert seq % TILE == 0 and dec_seq == CHUNK and (n_sample * dec_seq) % TILE == 0
    assert cache_k.shape[2] == WINDOW

    row = lambda p: p.reshape(1, -1)
    mixer_params = (
        row(ln_mix_g[0]), w_in[0].astype(BF16), row(ln_v_g[0]), row(ln_v_b[0]),
        w_s[0], jnp.broadcast_to(b_s[0][:, :, None], (GMLP_GROUPS, GMLP_CHUNK, LANES)),
        attn_sinks[0], _alibi_bias(),
        w_pa[0].astype(BF16), w_pb[0].astype(BF16), w_o[0].astype(BF16))
    ffn_params = (row(ln_ffn_g[0]), w_up[0].astype(BF16), w_conv[0], row(b_conv[0]),
                  w_down[0].astype(BF16), row(ln_final_g))

    xp = x_prompt.reshape(n_prompt * seq, D_MODEL)
    x1p, kp, vp = _mixer(xp, n_prompt, mixer_params, None)
    yp, convp = _ffn(x1p, n_prompt, ffn_params, None)

    xs = x_sample.reshape(n_sample * dec_seq, D_MODEL)
    caches = (cache_k[0].reshape(n_sample, WINDOW, KV_WIDTH), cache_v[0].reshape(n_sample, WINDOW, KV_WIDTH))
    x1s, ks, vs, gvs = _mixer(xs, n_sample, mixer_params, caches)
    ys, convs = _ffn(x1s, n_sample, ffn_params, cache_conv[0])

    kv_shape = lambda a: a.reshape(1, a.shape[0], a.shape[1], N_KV_HEADS, HEAD_DIM)
    return (yp.reshape(n_prompt, seq, D_MODEL), ys.reshape(n_sample, dec_seq, D_MODEL),
            kv_shape(kp), kv_shape(vp), convp[None],
            kv_shape(ks), kv_shape(vs), convs[None],
            gvs.reshape(1, n_sample, dec_seq, GMLP_WIDTH))
```

```python
import functools

import jax
import jax.numpy as jnp
from jax import lax
from jax.experimental import pallas as pl
from jax.experimental.pallas import tpu as pltpu

F32 = jnp.float32
BF16 = jnp.bfloat16

D_MODEL = 1024
GMLP_WIDTH = 1024
GMLP_GROUPS = 8
GMLP_CHUNK = 128
N_HEADS = 16
N_KV_HEADS = 4
GQA_GROUP = N_HEADS // N_KV_HEADS
HEAD_DIM = 64
ATTN_WIDTH = N_HEADS * HEAD_DIM
KV_WIDTH = N_KV_HEADS * HEAD_DIM
CHUNK = 64
WINDOW = 128
KEYS = WINDOW + CHUNK
D_FF = 2816
CONV_W = 3
EPS = 1e-6
NEG_INF = -1e30
COL_U = 0
COL_VG = COL_U + GMLP_WIDTH
COL_Q = COL_VG + GMLP_WIDTH
COL_K = COL_Q + ATTN_WIDTH
COL_V = COL_K + KV_WIDTH
COL_GATE = COL_V + KV_WIDTH
IN_COLS = COL_GATE + 2 * D_MODEL

LANES = 128
SUBLANES = 8
HALF = LANES // 2

TILE = 512
FF_CHUNK = 256
UNIT = 256
PROMPT_SLOTS = (
    (("m", 0),), (("m", 1),), (("m", 2),), (("m", 3),),
    (("o", 0), ("o", 1), ("g", 0)), (("o", 2), ("o", 3), ("g", 1)),
    (("g", 2), ("g", 3), ("g", 4)), (("g", 5), ("g", 6), ("g", 7)),
)
VMEM_LIMIT_BYTES = 52 * 1024 * 1024


def _rms_norm(x, g):
    return x * lax.rsqrt(jnp.mean(x * x, axis=-1, keepdims=True) + EPS) * g


def _layer_norm(x, g, b):
    mu = jnp.mean(x, axis=-1, keepdims=True)
    xc = x - mu
    return xc * lax.rsqrt(jnp.mean(xc * xc, axis=-1, keepdims=True) + EPS) * g + b


def _col(x, c):
    return x[:, c * LANES:(c + 1) * LANES]


def _swap_halves(x):
    n = x.shape[1] // LANES
    return jnp.concatenate([pltpu.roll(_col(x, c), HALF, axis=1) for c in range(n)], axis=1)


def _low_half_mask():
    return lax.broadcasted_iota(jnp.int32, (1, LANES), 1) < HALF


def _project(x, lng_ref, win_ref, lvg_ref, lvb_ref, h_scr, u_scr, vn_scr, q_scr):
    h_scr[...] = _rms_norm(x, lng_ref[...]).astype(BF16)

    def proj(c0, n):
        return jnp.dot(h_scr[...], win_ref[:, c0:c0 + n], preferred_element_type=F32)

    u_scr[...] = jax.nn.gelu(proj(COL_U, GMLP_WIDTH))
    vn = _layer_norm(jax.nn.gelu(proj(COL_VG, GMLP_WIDTH)), lvg_ref[...], lvb_ref[...])
    vn_scr[...] = vn.astype(BF16)
    q = proj(COL_Q, ATTN_WIDTH) * (HEAD_DIM ** -0.5)
    qs = _swap_halves(q)
    blocks = []
    for c in range(ATTN_WIDTH // LANES):
        blocks += [_col(q, c), _col(qs, c)]
    q_scr[...] = jnp.concatenate(blocks, axis=1).astype(BF16)
    kv = proj(COL_K, 2 * KV_WIDTH)
    return kv[:, :KV_WIDTH], kv[:, KV_WIDTH:], vn


def _kv_blocks(k, v):
    lo = _low_half_mask()
    ks, vs = _swap_halves(k), _swap_halves(v)
    kb, vb = [], []
    for j in range(N_KV_HEADS):
        c = j // 2
        if j % 2 == 0:
            kb.append(jnp.where(lo, _col(k, c), 0.0))
            vb.append(jnp.where(lo, _col(v, c), _col(vs, c)))
        else:
            kb.append(jnp.where(lo, _col(ks, c), 0.0))
            vb.append(jnp.where(lo, _col(vs, c), _col(v, c)))
    return jnp.concatenate(kb, axis=1).astype(BF16), jnp.concatenate(vb, axis=1).astype(BF16)


def _scores(qc, kctx):
    out = []
    for j in range(N_KV_HEADS):
        qs = jnp.concatenate([_col(qc, GQA_GROUP * j + g) for g in range(GQA_GROUP)], axis=0)
        out.append(lax.dot_general(qs, kctx(j), (((1,), (1,)), ((), ())), preferred_element_type=F32))
    return out


def _softmax_pv(scores, vctx, sink_ref, bias_ref, first_valid_key):
    lo = _low_half_mask()
    ones = jnp.ones((KEYS, LANES), BF16)
    outs = []
    for j in range(N_KV_HEADS):
        heads = [GQA_GROUP * j + g for g in range(GQA_GROUP)]
        s = scores[j]
        es, ms = [], []
        for g, h in enumerate(heads):
            sg = s[g * CHUNK:(g + 1) * CHUNK] - bias_ref[h]
            if first_valid_key is not None:
                kj = lax.broadcasted_iota(jnp.int32, (1, KEYS), 1)
                sg = jnp.where(kj >= first_valid_key, sg, NEG_INF)
            m = jnp.maximum(jnp.max(sg, axis=-1, keepdims=True), sink_ref[h])
            es.append(jnp.exp(sg - m).astype(BF16))
            ms.append(m)
        o = jnp.dot(jnp.concatenate(es, axis=0), jnp.concatenate([vctx(j), ones], axis=1),
                    preferred_element_type=F32)
        for p in range(GQA_GROUP // 2):
            g0, g1 = 2 * p, 2 * p + 1
            o0, o1 = o[g0 * CHUNK:(g0 + 1) * CHUNK], o[g1 * CHUNK:(g1 + 1) * CHUNK]
            num = jnp.where(lo, o0[:, :LANES], o1[:, :LANES])
            den = jnp.where(lo, o0[:, LANES:], o1[:, LANES:])
            sink_term = jnp.where(lo, jnp.exp(sink_ref[heads[g0]] - ms[g0]),
                                  jnp.exp(sink_ref[heads[g1]] - ms[g1]))
            outs.append(num / (den + sink_term))
    return outs


def _gate_unit(k, h_scr, win_ref, g_scr):
    cols = slice(k * UNIT, (k + 1) * UNIT)
    g_scr[:, cols] = jax.nn.sigmoid(
        jnp.dot(h_scr[...], win_ref[:, COL_GATE + k * UNIT:COL_GATE + (k + 1) * UNIT], preferred_element_type=F32))


def _merge_unit(i, g_scr, a_scr, b_scr, m_scr, wpa_ref, wpb_ref):
    cols = slice(i * UNIT, (i + 1) * UNIT)
    ya = jnp.dot(a_scr[...], wpa_ref[:, cols], preferred_element_type=F32)
    yb = jnp.dot(b_scr[...], wpb_ref[:, cols], preferred_element_type=F32)
    m_scr[:, cols] = (g_scr[:, cols] * ya + g_scr[:, D_MODEL + i * UNIT:D_MODEL + (i + 1) * UNIT] * yb).astype(BF16)


def _out_unit(i, x_ref, m_scr, wo_ref, x1_ref):
    cols = slice(i * UNIT, (i + 1) * UNIT)
    x1_ref[:, cols] = x_ref[:, cols] + jnp.dot(m_scr[...], wo_ref[:, cols], preferred_element_type=F32)


def _mixer_prompt_kernel(x_ref, xprev_ref, lng_ref, win_ref, lvg_ref, lvb_ref, ws_ref, bsb_ref, sink_ref, bias_ref,
                         wpa_ref, wpb_ref, wo_ref,
                         x1_ref, knew_ref, vnew_ref,
                         h_scr, u_scr, vn_scr, q_scr, k_scr, v_scr, a_scr, b_scr, m_scr, g_scr, a_new, b_new,
                         *, n_steps, tiles_per_seq):
    t = pl.program_id(0)
    seq_tile = lax.rem(jnp.minimum(t, n_steps - 2), tiles_per_seq)

    @pl.when(t == 0)
    def _():
        a_scr[...] = jnp.zeros(a_scr.shape, BF16)
        b_scr[...] = jnp.zeros(b_scr.shape, BF16)
        g_scr[...] = jnp.zeros(g_scr.shape, F32)
        k_scr[0:WINDOW, :] = jnp.zeros((WINDOW, N_KV_HEADS * LANES), BF16)
        v_scr[0:WINDOW, :] = jnp.zeros((WINDOW, N_KV_HEADS * LANES), BF16)

    def run_unit(kind, i):
        if kind == "m":
            _merge_unit(i, g_scr, a_scr, b_scr, m_scr, wpa_ref, wpb_ref)
        elif kind == "o":
            _out_unit(i, xprev_ref, m_scr, wo_ref, x1_ref)
        else:
            _gate_unit(i, h_scr, win_ref, g_scr)

    x = x_ref[...]
    k, v, _ = _project(x, lng_ref, win_ref, lvg_ref, lvb_ref, h_scr, u_scr, vn_scr, q_scr)
    knew_ref[0] = k[TILE - WINDOW:]
    vnew_ref[0] = v[TILE - WINDOW:]

    ri = lax.broadcasted_iota(jnp.int32, (GMLP_CHUNK, GMLP_CHUNK), 0)
    ci = lax.broadcasted_iota(jnp.int32, (GMLP_CHUNK, GMLP_CHUNK), 1)
    causal = jnp.where((ci < CHUNK) | (ri >= CHUNK), 1.0, 0.0).astype(F32)
    n_blk = TILE // GMLP_CHUNK
    for g in range(GMLP_GROUPS):
        wm = (ws_ref[g] * causal).astype(BF16)
        vcat = jnp.concatenate(
            [vn_scr[n * GMLP_CHUNK:(n + 1) * GMLP_CHUNK, g * LANES:(g + 1) * LANES] for n in range(n_blk)], axis=1)
        sp = jnp.dot(wm, vcat, preferred_element_type=F32)
        for n in range(n_blk):
            rows = slice(n * GMLP_CHUNK, (n + 1) * GMLP_CHUNK)
            cols = slice(g * LANES, (g + 1) * LANES)
            a_new[rows, cols] = (u_scr[rows, cols] * (_col(sp, n) + bsb_ref[g])).astype(BF16)

    kb, vb = _kv_blocks(k, v)
    k_scr[WINDOW:, :] = kb
    v_scr[WINDOW:, :] = vb
    for c in range(TILE // CHUNK):
        r0 = c * CHUNK
        first_valid = None
        if c * CHUNK < WINDOW:
            first_valid = jnp.where(seq_tile == 0, WINDOW - c * CHUNK, 0)
        scores = _scores(q_scr[r0:r0 + CHUNK, :], lambda j: k_scr[r0:r0 + KEYS, j * LANES:(j + 1) * LANES])
        for kind, i in PROMPT_SLOTS[c]:
            run_unit(kind, i)
        outs = _softmax_pv(scores, lambda j: v_scr[r0:r0 + KEYS, j * LANES:(j + 1) * LANES],
                           sink_ref, bias_ref, first_valid)
        b_new[r0:r0 + CHUNK, :] = jnp.concatenate(outs, axis=1).astype(BF16)
    k_scr[0:WINDOW, :] = k_scr[TILE:TILE + WINDOW, :]
    v_scr[0:WINDOW, :] = v_scr[TILE:TILE + WINDOW, :]
    a_scr[...] = a_new[...]
    b_scr[...] = b_new[...]


def _mixer_sample_kernel(x_ref, lng_ref, win_ref, lvg_ref, lvb_ref, ws_ref, bsb_ref, sink_ref, bias_ref,
                         wpa_ref, wpb_ref, wo_ref, ck_ref, cv_ref,
                         x1_ref, knew_ref, vnew_ref, gv_ref,
                         h_scr, u_scr, vn_scr, q_scr, k_scr, v_scr, a_scr, b_scr, m_scr, g_scr):
    n_seq = TILE // CHUNK
    x = x_ref[...]
    k, v, vn = _project(x, lng_ref, win_ref, lvg_ref, lvb_ref, h_scr, u_scr, vn_scr, q_scr)
    gv_ref[...] = vn
    for c in range(n_seq):
        knew_ref[c] = k[c * CHUNK:(c + 1) * CHUNK]
        vnew_ref[c] = v[c * CHUNK:(c + 1) * CHUNK]

    for g in range(GMLP_GROUPS):
        wm = ws_ref[g][:CHUNK, :CHUNK].astype(BF16)
        vcat = jnp.concatenate(
            [vn_scr[c * CHUNK:(c + 1) * CHUNK, g * LANES:(g + 1) * LANES] for c in range(n_seq)], axis=1)
        sp = jnp.dot(wm, vcat, preferred_element_type=F32)
        bs = bsb_ref[g][:CHUNK]
        for c in range(n_seq):
            rows = slice(c * CHUNK, (c + 1) * CHUNK)
            cols = slice(g * LANES, (g + 1) * LANES)
            a_scr[rows, cols] = (u_scr[rows, cols] * (_col(sp, c) + bs)).astype(BF16)

    kb, vb = _kv_blocks(k, v)
    for c in range(n_seq):
        ckb, cvb = _kv_blocks(ck_ref[c], cv_ref[c])
        k_scr[c, 0:WINDOW, :] = ckb
        v_scr[c, 0:WINDOW, :] = cvb
        k_scr[c, WINDOW:, :] = kb[c * CHUNK:(c + 1) * CHUNK]
        v_scr[c, WINDOW:, :] = vb[c * CHUNK:(c + 1) * CHUNK]
    for c in range(n_seq):
        r0 = c * CHUNK
        scores = _scores(q_scr[r0:r0 + CHUNK, :], lambda j: k_scr[c, :, j * LANES:(j + 1) * LANES])
        outs = _softmax_pv(scores, lambda j: v_scr[c, :, j * LANES:(j + 1) * LANES], sink_ref, bias_ref, None)
        b_scr[r0:r0 + CHUNK, :] = jnp.concatenate(outs, axis=1).astype(BF16)

    for kk in range(2 * D_MODEL // UNIT):
        _gate_unit(kk, h_scr, win_ref, g_scr)
    for i in range(D_MODEL // UNIT):
        _merge_unit(i, g_scr, a_scr, b_scr, m_scr, wpa_ref, wpb_ref)
    for i in range(D_MODEL // UNIT):
        _out_unit(i, x_ref, m_scr, wo_ref, x1_ref)


def _ffn_kernel(*refs, sample):
    if sample:
        (x1_ref, lnf_ref, wup_ref, wconv_ref, bconv_ref, wdown_ref, lnfin_ref, cconv_ref,
         y_ref, cnew_ref, h2_scr, g_scr, a_buf) = refs
        seq_len = CHUNK
    else:
        (x1_ref, lnf_ref, wup_ref, wconv_ref, bconv_ref, wdown_ref, lnfin_ref,
         y_ref, cnew_ref, h2_scr, g_scr, a_buf, hist_scr) = refs
        seq_len = TILE

        @pl.when(pl.program_id(1) == 0)
        def _():
            hist_scr[...] = jnp.zeros(hist_scr.shape, F32)

    n_seq = TILE // seq_len
    x1 = x1_ref[...]
    h2_scr[...] = _rms_norm(x1, lnf_ref[...]).astype(BF16)
    for c0 in range(0, D_FF, FF_CHUNK):
        cols = slice(c0, c0 + FF_CHUNK)
        a = jnp.dot(h2_scr[...], wup_ref[:, cols], preferred_element_type=F32)
        val = jnp.dot(h2_scr[...], wup_ref[:, D_FF + c0:D_FF + c0 + FF_CHUNK], preferred_element_type=F32)
        w = wconv_ref[:, cols]
        taps = []
        for s in range(n_seq):
            a_s = a[s * seq_len:(s + 1) * seq_len]
            if sample:
                a_buf[s, SUBLANES - (CONV_W - 1):SUBLANES, :] = cconv_ref[s, :, cols]
            else:
                a_buf[s, 0:SUBLANES, :] = hist_scr[:, cols]
                hist_scr[:, cols] = a_s[seq_len - SUBLANES:]
            a_buf[s, SUBLANES:, :] = a_s
            cnew_ref[s, :, cols] = a_s[seq_len - (CONV_W - 1):]
            taps.append((w[0:1] * a_buf[s, SUBLANES - 2:SUBLANES - 2 + seq_len, :]
                         + w[1:2] * a_buf[s, SUBLANES - 1:SUBLANES - 1 + seq_len, :])
                        + w[2:3] * a_s)
        conv = bconv_ref[:, cols] + (taps[0] if n_seq == 1 else jnp.concatenate(taps, axis=0))
        g_scr[:, cols] = (jax.nn.gelu(conv) * val).astype(BF16)
    x2 = x1 + jnp.dot(g_scr[...], wdown_ref[...], preferred_element_type=F32)
    y_ref[...] = _rms_norm(x2, lnfin_ref[...])


def _resident(shape):
    nd = len(shape)
    return pl.BlockSpec(shape, lambda *_: (0,) * nd, pipeline_mode=pl.Buffered(1))


def _alibi_bias():
    slopes = jnp.exp2(-8.0 * jnp.arange(1, N_HEADS + 1, dtype=F32) / N_HEADS)
    qi = jnp.arange(CHUNK)[:, None]
    kj = jnp.arange(KEYS)[None, :]
    dist = jnp.abs(qi + WINDOW - kj).astype(F32)
    return slopes[:, None, None] * dist[None]


def _mixer(x2d, n_batch, params, caches):
    lng, win, lvg, lvb, ws, bsb, sinks, bias, wpa, wpb, wo = params
    tokens = x2d.shape[0]
    sample = caches is not None
    common_in = [
        None,
        _resident(lng.shape), _resident(win.shape), _resident(lvg.shape), _resident(lvb.shape),
        _resident(ws.shape), _resident(bsb.shape),
        pl.BlockSpec(memory_space=pltpu.SMEM),
        _resident(bias.shape), _resident(wpa.shape), _resident(wpb.shape), _resident(wo.shape),
    ]
    common_scratch = [
        pltpu.VMEM((TILE, D_MODEL), BF16),
        pltpu.VMEM((TILE, GMLP_WIDTH), F32),
        pltpu.VMEM((TILE, GMLP_WIDTH), BF16),
        pltpu.VMEM((TILE, N_HEADS * LANES), BF16),
    ]
    tail_scratch = [
        pltpu.VMEM((TILE, GMLP_WIDTH), BF16),
        pltpu.VMEM((TILE, ATTN_WIDTH), BF16),
        pltpu.VMEM((TILE, D_MODEL), BF16),
        pltpu.VMEM((TILE, 2 * D_MODEL), F32),
    ]
    if not sample:
        tiles_per_seq = tokens // n_batch // TILE
        n_tiles = n_batch * tiles_per_seq
        grid = (n_tiles + 1,)
        cur = lambda t: jnp.minimum(t, n_tiles - 1)
        prev = lambda t: jnp.maximum(t - 1, 0)
        tok_cur = pl.BlockSpec((TILE, D_MODEL), lambda t: (cur(t), 0))
        tok_prev = pl.BlockSpec((TILE, D_MODEL), lambda t: (prev(t), 0))
        kv_out = pl.BlockSpec((1, WINDOW, KV_WIDTH), lambda t: (cur(t) // tiles_per_seq, 0, 0))
        in_specs = [tok_cur, tok_prev] + common_in[1:]
        out_specs = [tok_prev, kv_out, kv_out]
        out_shape = [jax.ShapeDtypeStruct((tokens, D_MODEL), F32),
                     jax.ShapeDtypeStruct((n_batch, WINDOW, KV_WIDTH), F32),
                     jax.ShapeDtypeStruct((n_batch, WINDOW, KV_WIDTH), F32)]
        kv_scratch = [pltpu.VMEM((WINDOW + TILE, N_KV_HEADS * LANES), BF16)] * 2
        tail_scratch = tail_scratch + [pltpu.VMEM((TILE, GMLP_WIDTH), BF16),
                                       pltpu.VMEM((TILE, ATTN_WIDTH), BF16)]
        body = functools.partial(_mixer_prompt_kernel, n_steps=n_tiles + 1, tiles_per_seq=tiles_per_seq)
        args, lead, name = (), (x2d, x2d), "mixer_prompt"
    else:
        n_seq = TILE // CHUNK
        grid = (tokens // TILE,)
        tok = pl.BlockSpec((TILE, D_MODEL), lambda t: (t, 0))
        cache_spec = pl.BlockSpec((n_seq, WINDOW, KV_WIDTH), lambda t: (t, 0, 0))
        kv_out = pl.BlockSpec((n_seq, CHUNK, KV_WIDTH), lambda t: (t, 0, 0))
        in_specs = [tok] + common_in[1:] + [cache_spec, cache_spec]
        out_specs = [tok, kv_out, kv_out, pl.BlockSpec((TILE, GMLP_WIDTH), lambda t: (t, 0))]
        out_shape = [jax.ShapeDtypeStruct((tokens, D_MODEL), F32),
                     jax.ShapeDtypeStruct((n_batch, CHUNK, KV_WIDTH), F32),
                     jax.ShapeDtypeStruct((n_batch, CHUNK, KV_WIDTH), F32),
                     jax.ShapeDtypeStruct((tokens, GMLP_WIDTH), F32)]
        kv_scratch = [pltpu.VMEM((n_seq, KEYS, N_KV_HEADS * LANES), BF16)] * 2
        body, args, lead, name = _mixer_sample_kernel, tuple(caches), (x2d,), "mixer_sample"
    return pl.pallas_call(
        body,
        grid=grid,
        in_specs=in_specs,
        out_specs=out_specs,
        out_shape=out_shape,
        scratch_shapes=common_scratch + kv_scratch + tail_scratch,
        compiler_params=pltpu.CompilerParams(
            dimension_semantics=("arbitrary",) * len(grid), vmem_limit_bytes=VMEM_LIMIT_BYTES),
        name=name,
    )(*lead, lng, win, lvg, lvb, ws, bsb, sinks, bias, wpa, wpb, wo, *args)


def _ffn(x2d, n_batch, params, cache_conv):
    lnf, wup, wconv, bconv, wdown, lnfin = params
    tokens = x2d.shape[0]
    sample = cache_conv is not None
    weights_in = [_resident(lnf.shape), _resident(wup.shape), _resident(wconv.shape), _resident(bconv.shape),
                  _resident(wdown.shape), _resident(lnfin.shape)]
    scratch = [pltpu.VMEM((TILE, D_MODEL), BF16),
               pltpu.VMEM((TILE, D_FF), BF16)]
    if not sample:
        n_tiles = tokens // n_batch // TILE
        grid = (n_batch, n_tiles)
        tok = pl.BlockSpec((TILE, D_MODEL), lambda b, s: (b * n_tiles + s, 0))
        in_specs = [tok] + weights_in
        conv_out = pl.BlockSpec((1, CONV_W - 1, D_FF), lambda b, s: (b, 0, 0))
        scratch += [pltpu.VMEM((1, SUBLANES + TILE, FF_CHUNK), F32),
                    pltpu.VMEM((SUBLANES, D_FF), F32)]
        args, name = (), "ffn_prompt"
    else:
        n_seq = TILE // CHUNK
        grid = (tokens // TILE,)
        tok = pl.BlockSpec((TILE, D_MODEL), lambda t: (t, 0))
        conv_out = pl.BlockSpec((n_seq, CONV_W - 1, D_FF), lambda t: (t, 0, 0))
        in_specs = [tok] + weights_in + [conv_out]
        scratch += [pltpu.VMEM((n_seq, SUBLANES + CHUNK, FF_CHUNK), F32)]
        args, name = (cache_conv,), "ffn_sample"
    return pl.pallas_call(
        functools.partial(_ffn_kernel, sample=sample),
        grid=grid,
        in_specs=in_specs,
        out_specs=[tok, conv_out],
        out_shape=[jax.ShapeDtypeStruct((tokens, D_MODEL), F32),
                   jax.ShapeDtypeStruct((n_batch, CONV_W - 1, D_FF), F32)],
        scratch_shapes=scratch,
        compiler_params=pltpu.CompilerParams(
            dimension_semantics=("arbitrary",) * len(grid), vmem_limit_bytes=VMEM_LIMIT_BYTES),
        name=name,
    )(x2d, lnf, wup, wconv, bconv, wdown, lnfin, *args)


def kernel(x_prompt, x_sample, cache_k, cache_v, cache_conv, ln_mix_g, w_in, ln_v_g, ln_v_b, w_s, b_s,
           attn_sinks, w_pa, w_pb, w_o, ln_ffn_g, w_up, w_conv, b_conv, w_down, ln_final_g):
    depth = w_in.shape[0]
    assert depth == 1, "single-layer kernel"
    n_prompt, seq, _ = x_prompt.shape
    n_sample, dec_seq, _ = x_sample.shape
    assert seq % TILE == 0 and dec_seq == CHUNK and (n_sample * dec_seq) % TILE == 0
    assert cache_k.shape[2] == WINDOW

    row = lambda p: p.reshape(1, -1)
    mixer_params = (
        row(ln_mix_g[0]), w_in[0].astype(BF16), row(ln_v_g[0]), row(ln_v_b[0]),
        w_s[0], jnp.broadcast_to(b_s[0][:, :, None], (GMLP_GROUPS, GMLP_CHUNK, LANES)),
        attn_sinks[0], _alibi_bias(),
        w_pa[0].astype(BF16), w_pb[0].astype(BF16), w_o[0].astype(BF16))
    ffn_params = (row(ln_ffn_g[0]), w_up[0].astype(BF16), w_conv[0], row(b_conv[0]),
                  w_down[0].astype(BF16), row(ln_final_g))

    xp = x_prompt.reshape(n_prompt * seq, D_MODEL)
    x1p, kp, vp = _mixer(xp, n_prompt, mixer_params, None)
    yp, convp = _ffn(x1p, n_prompt, ffn_params, None)

    xs = x_sample.reshape(n_sample * dec_seq, D_MODEL)
    caches = (cache_k[0].reshape(n_sample, WINDOW, KV_WIDTH), cache_v[0].reshape(n_sample, WINDOW, KV_WIDTH))
    x1s, ks, vs, gvs = _mixer(xs, n_sample, mixer_params, caches)
    ys, convs = _ffn(x1s, n_sample, ffn_params, cache_conv[0])

    kv_shape = lambda a: a.reshape(1, a.shape[0], a.shape[1], N_KV_HEADS, HEAD_DIM)
    return (yp.reshape(n_prompt, seq, D_MODEL), ys.reshape(n_sample, dec_seq, D_MODEL),
            kv_shape(kp), kv_shape(vp), convp[None],
            kv_shape(ks), kv_shape(vs), convs[None],
            gvs.reshape(1, n_sample, dec_seq, GMLP_WIDTH))
```

```python
import functools

import jax
import jax.numpy as jnp
from jax import lax
from jax.experimental import pallas as pl
from jax.experimental.pallas import tpu as pltpu

F32 = jnp.float32
BF16 = jnp.bfloat16

D_MODEL = 1024
GMLP_WIDTH = 1024
GMLP_GROUPS = 8
GMLP_CHUNK = 128
N_HEADS = 16
N_KV_HEADS = 4
GQA_GROUP = N_HEADS // N_KV_HEADS
HEAD_DIM = 64
ATTN_WIDTH = N_HEADS * HEAD_DIM
KV_WIDTH = N_KV_HEADS * HEAD_DIM
CHUNK = 64
WINDOW = 128
KEYS = WINDOW + CHUNK
D_FF = 2816
CONV_W = 3
EPS = 1e-6
NEG_INF = -1e30
COL_U = 0
COL_VG = COL_U + GMLP_WIDTH
COL_Q = COL_VG + GMLP_WIDTH
COL_K = COL_Q + ATTN_WIDTH
COL_V = COL_K + KV_WIDTH
COL_GATE = COL_V + KV_WIDTH
IN_COLS = COL_GATE + 2 * D_MODEL

LANES = 128
SUBLANES = 8
HALF = LANES // 2

TILE = 512
FF_CHUNK = 256
UNIT = 256
PROMPT_SLOTS = (
    (("m", 0),), (("m", 1),), (("m", 2),), (("m", 3),),
    (("o", 0), ("o", 1), ("g", 0)), (("o", 2), ("o", 3), ("g", 1)),
    (("g", 2), ("g", 3), ("g", 4)), (("g", 5), ("g", 6), ("g", 7)),
)
FFN_TILE = 1024
FFN_EDGE_ROWS = 256
VMEM_LIMIT_BYTES = 52 * 1024 * 1024


def _rms_norm(x, g):
    return x * lax.rsqrt(jnp.mean(x * x, axis=-1, keepdims=True) + EPS) * g


def _layer_norm(x, g, b):
    mu = jnp.mean(x, axis=-1, keepdims=True)
    xc = x - mu
    return xc * lax.rsqrt(jnp.mean(xc * xc, axis=-1, keepdims=True) + EPS) * g + b


def _col(x, c):
    return x[:, c * LANES:(c + 1) * LANES]


def _swap_halves(x):
    n = x.shape[1] // LANES
    return jnp.concatenate([pltpu.roll(_col(x, c), HALF, axis=1) for c in range(n)], axis=1)


def _low_half_mask():
    return lax.broadcasted_iota(jnp.int32, (1, LANES), 1) < HALF


def _project(x_ref, lng_ref, win_ref, lvg_ref, lvb_ref, h_scr, u_scr, vn_scr, q_scr):
    h_scr[...] = _rms_norm(x_ref[...], lng_ref[...]).astype(BF16)

    def proj(c0, n):
        return jnp.dot(h_scr[...], win_ref[:, c0:c0 + n], preferred_element_type=F32)

    u_scr[...] = jax.nn.gelu(proj(COL_U, GMLP_WIDTH))
    vn = _layer_norm(jax.nn.gelu(proj(COL_VG, GMLP_WIDTH)), lvg_ref[...], lvb_ref[...])
    vn_scr[...] = vn.astype(BF16)
    q = proj(COL_Q, ATTN_WIDTH) * (HEAD_DIM ** -0.5)
    qs = _swap_halves(q)
    blocks = []
    for c in range(ATTN_WIDTH // LANES):
        blocks += [_col(q, c), _col(qs, c)]
    q_scr[...] = jnp.concatenate(blocks, axis=1).astype(BF16)
    kv = proj(COL_K, 2 * KV_WIDTH)
    return kv[:, :KV_WIDTH], kv[:, KV_WIDTH:], vn


def _kv_blocks(k, v):
    lo = _low_half_mask()
    ks, vs = _swap_halves(k), _swap_halves(v)
    kb, vb = [], []
    for j in range(N_KV_HEADS):
        c = j // 2
        if j % 2 == 0:
            kb.append(jnp.where(lo, _col(k, c), 0.0))
            vb.append(jnp.where(lo, _col(v, c), _col(vs, c)))
        else:
            kb.append(jnp.where(lo, _col(ks, c), 0.0))
            vb.append(jnp.where(lo, _col(vs, c), _col(v, c)))
    return jnp.concatenate(kb, axis=1).astype(BF16), jnp.concatenate(vb, axis=1).astype(BF16)


def _scores(qc, kctx):
    out = []
    for j in range(N_KV_HEADS):
        qs = jnp.concatenate([_col(qc, GQA_GROUP * j + g) for g in range(GQA_GROUP)], axis=0)
        out.append(lax.dot_general(qs, kctx(j), (((1,), (1,)), ((), ())), preferred_element_type=F32))
    return out


def _softmax_pv(scores, vctx, sink_ref, bias_ref, first_valid_key):
    lo = _low_half_mask()
    ones = jnp.ones((KEYS, LANES), BF16)
    outs = []
    for j in range(N_KV_HEADS):
        heads = [GQA_GROUP * j + g for g in range(GQA_GROUP)]
        s = scores[j]
        es, ms = [], []
        for g, h in enumerate(heads):
            sg = s[g * CHUNK:(g + 1) * CHUNK] - bias_ref[h]
            if first_valid_key is not None:
                kj = lax.broadcasted_iota(jnp.int32, (1, KEYS), 1)
                sg = jnp.where(kj >= first_valid_key, sg, NEG_INF)
            m = jnp.maximum(jnp.max(sg, axis=-1, keepdims=True), sink_ref[h])
            es.append(jnp.exp(sg - m).astype(BF16))
            ms.append(m)
        o = jnp.dot(jnp.concatenate(es, axis=0), jnp.concatenate([vctx(j), ones], axis=1),
                    preferred_element_type=F32)
        for p in range(GQA_GROUP // 2):
            g0, g1 = 2 * p, 2 * p + 1
            o0, o1 = o[g0 * CHUNK:(g0 + 1) * CHUNK], o[g1 * CHUNK:(g1 + 1) * CHUNK]
            num = jnp.where(lo, o0[:, :LANES], o1[:, :LANES])
            den = jnp.where(lo, o0[:, LANES:], o1[:, LANES:])
            sink_term = jnp.where(lo, jnp.exp(sink_ref[heads[g0]] - ms[g0]),
                                  jnp.exp(sink_ref[heads[g1]] - ms[g1]))
            outs.append(num / (den + sink_term))
    return outs


def _gate_unit(k, h_scr, win_ref, g_scr):
    cols = slice(k * UNIT, (k + 1) * UNIT)
    g_scr[:, cols] = jax.nn.sigmoid(
        jnp.dot(h_scr[...], win_ref[:, COL_GATE + k * UNIT:COL_GATE + (k + 1) * UNIT], preferred_element_type=F32))


def _merge_unit(i, g_scr, a_scr, b_scr, m_scr, wpa_ref, wpb_ref):
    cols = slice(i * UNIT, (i + 1) * UNIT)
    ya = jnp.dot(a_scr[...], wpa_ref[:, cols], preferred_element_type=F32)
    yb = jnp.dot(b_scr[...], wpb_ref[:, cols], preferred_element_type=F32)
    m_scr[:, cols] = (g_scr[:, cols] * ya + g_scr[:, D_MODEL + i * UNIT:D_MODEL + (i + 1) * UNIT] * yb).astype(BF16)


def _out_unit(i, x_ref, m_scr, wo_ref, x1_ref):
    cols = slice(i * UNIT, (i + 1) * UNIT)
    x1_ref[:, cols] = x_ref[:, cols] + jnp.dot(m_scr[...], wo_ref[:, cols], preferred_element_type=F32)


def _mixer_prompt_kernel(x_ref, xprev_ref, lng_ref, win_ref, lvg_ref, lvb_ref, ws_ref, bsb_ref, sink_ref, bias_ref,
                         wpa_ref, wpb_ref, wo_ref,
                         x1_ref, knew_ref, vnew_ref,
                         h_scr, u_scr, vn_scr, q_scr, k_scr, v_scr, a_scr, b_scr, m_scr, g_scr, a_new, b_new,
                         *, n_steps, tiles_per_seq):
    t = pl.program_id(0)
    seq_tile = lax.rem(jnp.minimum(t, n_steps - 2), tiles_per_seq)

    @pl.when(t == 0)
    def _():
        a_scr[...] = jnp.zeros(a_scr.shape, BF16)
        b_scr[...] = jnp.zeros(b_scr.shape, BF16)
        g_scr[...] = jnp.zeros(g_scr.shape, F32)
        k_scr[0:WINDOW, :] = jnp.zeros((WINDOW, N_KV_HEADS * LANES), BF16)
        v_scr[0:WINDOW, :] = jnp.zeros((WINDOW, N_KV_HEADS * LANES), BF16)

    def run_unit(kind, i):
        if kind == "m":
            _merge_unit(i, g_scr, a_scr, b_scr, m_scr, wpa_ref, wpb_ref)
        elif kind == "o":
            _out_unit(i, xprev_ref, m_scr, wo_ref, x1_ref)
        else:
            _gate_unit(i, h_scr, win_ref, g_scr)

    k, v, _ = _project(x_ref, lng_ref, win_ref, lvg_ref, lvb_ref, h_scr, u_scr, vn_scr, q_scr)
    knew_ref[0] = k[TILE - WINDOW:]
    vnew_ref[0] = v[TILE - WINDOW:]

    ri = lax.broadcasted_iota(jnp.int32, (GMLP_CHUNK, GMLP_CHUNK), 0)
    ci = lax.broadcasted_iota(jnp.int32, (GMLP_CHUNK, GMLP_CHUNK), 1)
    causal = jnp.where((ci < CHUNK) | (ri >= CHUNK), 1.0, 0.0).astype(F32)
    n_blk = TILE // GMLP_CHUNK
    for g in range(GMLP_GROUPS):
        wm = (ws_ref[g] * causal).astype(BF16)
        vcat = jnp.concatenate(
            [vn_scr[n * GMLP_CHUNK:(n + 1) * GMLP_CHUNK, g * LANES:(g + 1) * LANES] for n in range(n_blk)], axis=1)
        sp = jnp.dot(wm, vcat, preferred_element_type=F32)
        for n in range(n_blk):
            rows = slice(n * GMLP_CHUNK, (n + 1) * GMLP_CHUNK)
            cols = slice(g * LANES, (g + 1) * LANES)
            a_new[rows, cols] = (u_scr[rows, cols] * (_col(sp, n) + bsb_ref[g])).astype(BF16)

    kb, vb = _kv_blocks(k, v)
    k_scr[WINDOW:, :] = kb
    v_scr[WINDOW:, :] = vb
    for c in range(TILE // CHUNK):
        r0 = c * CHUNK
        first_valid = None
        if c * CHUNK < WINDOW:
            first_valid = jnp.where(seq_tile == 0, WINDOW - c * CHUNK, 0)
        scores = _scores(q_scr[r0:r0 + CHUNK, :], lambda j: k_scr[r0:r0 + KEYS, j * LANES:(j + 1) * LANES])
        for kind, i in PROMPT_SLOTS[c]:
            run_unit(kind, i)
        outs = _softmax_pv(scores, lambda j: v_scr[r0:r0 + KEYS, j * LANES:(j + 1) * LANES],
                           sink_ref, bias_ref, first_valid)
        b_new[r0:r0 + CHUNK, :] = jnp.concatenate(outs, axis=1).astype(BF16)
    k_scr[0:WINDOW, :] = k_scr[TILE:TILE + WINDOW, :]
    v_scr[0:WINDOW, :] = v_scr[TILE:TILE + WINDOW, :]
    a_scr[...] = a_new[...]
    b_scr[...] = b_new[...]


def _mixer_sample_kernel(x_ref, lng_ref, win_ref, lvg_ref, lvb_ref, ws_ref, bsb_ref, sink_ref, bias_ref,
                         wpa_ref, wpb_ref, wo_ref, ck_ref, cv_ref,
                         x1_ref, knew_ref, vnew_ref, gv_ref,
                         h_scr, u_scr, vn_scr, q_scr, k_scr, v_scr, a_scr, b_scr, m_scr, g_scr):
    n_seq = TILE // CHUNK
    k, v, vn = _project(x_ref, lng_ref, win_ref, lvg_ref, lvb_ref, h_scr, u_scr, vn_scr, q_scr)
    gv_ref[...] = vn
    for c in range(n_seq):
        knew_ref[c] = k[c * CHUNK:(c + 1) * CHUNK]
        vnew_ref[c] = v[c * CHUNK:(c + 1) * CHUNK]

    for g in range(GMLP_GROUPS):
        wm = ws_ref[g][:CHUNK, :CHUNK].astype(BF16)
        vcat = jnp.concatenate(
            [vn_scr[c * CHUNK:(c + 1) * CHUNK, g * LANES:(g + 1) * LANES] for c in range(n_seq)], axis=1)
        sp = jnp.dot(wm, vcat, preferred_element_type=F32)
        bs = bsb_ref[g][:CHUNK]
        for c in range(n_seq):
            rows = slice(c * CHUNK, (c + 1) * CHUNK)
            cols = slice(g * LANES, (g + 1) * LANES)
            a_scr[rows, cols] = (u_scr[rows, cols] * (_col(sp, c) + bs)).astype(BF16)

    kb, vb = _kv_blocks(k, v)
    for c in range(n_seq):
        ckb, cvb = _kv_blocks(ck_ref[c], cv_ref[c])
        k_scr[c, 0:WINDOW, :] = ckb
        v_scr[c, 0:WINDOW, :] = cvb
        k_scr[c, WINDOW:, :] = kb[c * CHUNK:(c + 1) * CHUNK]
        v_scr[c, WINDOW:, :] = vb[c * CHUNK:(c + 1) * CHUNK]
    for c in range(n_seq):
        r0 = c * CHUNK
        scores = _scores(q_scr[r0:r0 + CHUNK, :], lambda j: k_scr[c, :, j * LANES:(j + 1) * LANES])
        _gate_unit(c, h_scr, win_ref, g_scr)
        outs = _softmax_pv(scores, lambda j: v_scr[c, :, j * LANES:(j + 1) * LANES], sink_ref, bias_ref, None)
        b_scr[r0:r0 + CHUNK, :] = jnp.concatenate(outs, axis=1).astype(BF16)

    assert n_seq == 2 * D_MODEL // UNIT
    for i in range(D_MODEL // UNIT):
        _merge_unit(i, g_scr, a_scr, b_scr, m_scr, wpa_ref, wpb_ref)
    for i in range(D_MODEL // UNIT):
        _out_unit(i, x_ref, m_scr, wo_ref, x1_ref)


def _up_dots(h2_scr, wup_ref, c0):
    a = jnp.dot(h2_scr[...], wup_ref[:, c0:c0 + FF_CHUNK], preferred_element_type=F32)
    val = jnp.dot(h2_scr[...], wup_ref[:, D_FF + c0:D_FF + c0 + FF_CHUNK], preferred_element_type=F32)
    return a, val


def _conv_taps(a_s, buf, w):
    n = a_s.shape[0]
    return (w[0:1] * buf[SUBLANES - 2:SUBLANES - 2 + n, :] + w[1:2] * buf[SUBLANES - 1:SUBLANES - 1 + n, :]) \
        + w[2:3] * a_s


def _ffn_kernel(*refs, sample):
    if sample:
        (x1_ref, lnf_ref, wup_ref, wconv_ref, bconv_ref, wdown_ref, lnfin_ref, cconv_ref,
         y_ref, cnew_ref, h2_scr, g_scr, a_buf) = refs
        seq_len = CHUNK
    else:
        (x1_ref, lnf_ref, wup_ref, wconv_ref, bconv_ref, wdown_ref, lnfin_ref,
         y_ref, cnew_ref, h2_scr, g_scr, a_buf, hist_scr) = refs
        seq_len = FFN_TILE

        @pl.when(pl.program_id(1) == 0)
        def _():
            hist_scr[...] = jnp.zeros(hist_scr.shape, F32)

    n_seq = FFN_TILE // seq_len
    row_blocks = [slice(r, r + FFN_EDGE_ROWS) for r in range(0, FFN_TILE, FFN_EDGE_ROWS)]
    for rows in row_blocks:
        h2_scr[rows, :] = _rms_norm(x1_ref[rows, :], lnf_ref[...]).astype(BF16)
    for j, c0 in enumerate(range(0, D_FF, FF_CHUNK)):
        cols = slice(c0, c0 + FF_CHUNK)
        if j == 0:
            parts = [_up_dots(h2_scr.at[rows, :], wup_ref, c0) for rows in row_blocks]
            a = jnp.concatenate([p[0] for p in parts], axis=0)
            val = jnp.concatenate([p[1] for p in parts], axis=0)
        else:
            a, val = _up_dots(h2_scr, wup_ref, c0)
        w = wconv_ref[:, cols]
        taps = []
        for s in range(n_seq):
            a_s = a[s * seq_len:(s + 1) * seq_len]
            buf = a_buf.at[s]
            if sample:
                buf[SUBLANES - (CONV_W - 1):SUBLANES, :] = cconv_ref[s, :, cols]
            else:
                buf[0:SUBLANES, :] = hist_scr[:, cols]
                hist_scr[:, cols] = a_s[seq_len - SUBLANES:]
            buf[SUBLANES:, :] = a_s
            cnew_ref[s, :, cols] = a_s[seq_len - (CONV_W - 1):]
            taps.append(_conv_taps(a_s, buf, w))
        conv = bconv_ref[:, cols] + (taps[0] if n_seq == 1 else jnp.concatenate(taps, axis=0))
        g_scr[:, cols] = (jax.nn.gelu(conv) * val).astype(BF16)
    for rows in row_blocks:
        x2 = x1_ref[rows, :] + jnp.dot(g_scr[rows, :], wdown_ref[...], preferred_element_type=F32)
        y_ref[rows, :] = _rms_norm(x2, lnfin_ref[...])


def _resident(shape):
    nd = len(shape)
    return pl.BlockSpec(shape, lambda *_: (0,) * nd, pipeline_mode=pl.Buffered(1))


def _alibi_bias():
    slopes = jnp.exp2(-8.0 * jnp.arange(1, N_HEADS + 1, dtype=F32) / N_HEADS)
    qi = jnp.arange(CHUNK)[:, None]
    kj = jnp.arange(KEYS)[None, :]
    dist = jnp.abs(qi + WINDOW - kj).astype(F32)
    return slopes[:, None, None] * dist[None]


def _mixer(x2d, n_batch, params, caches):
    lng, win, lvg, lvb, ws, bsb, sinks, bias, wpa, wpb, wo = params
    tokens = x2d.shape[0]
    sample = caches is not None
    common_in = [
        None,
        _resident(lng.shape), _resident(win.shape), _resident(lvg.shape), _resident(lvb.shape),
        _resident(ws.shape), _resident(bsb.shape),
        pl.BlockSpec(memory_space=pltpu.SMEM),
        _resident(bias.shape), _resident(wpa.shape), _resident(wpb.shape), _resident(wo.shape),
    ]
    common_scratch = [
        pltpu.VMEM((TILE, D_MODEL), BF16),
        pltpu.VMEM((TILE, GMLP_WIDTH), F32),
        pltpu.VMEM((TILE, GMLP_WIDTH), BF16),
        pltpu.VMEM((TILE, N_HEADS * LANES), BF16),
    ]
    tail_scratch = [
        pltpu.VMEM((TILE, GMLP_WIDTH), BF16),
        pltpu.VMEM((TILE, ATTN_WIDTH), BF16),
        pltpu.VMEM((TILE, D_MODEL), BF16),
        pltpu.VMEM((TILE, 2 * D_MODEL), F32),
    ]
    if not sample:
        tiles_per_seq = tokens // n_batch // TILE
        n_tiles = n_batch * tiles_per_seq
        grid = (n_tiles + 1,)
        cur = lambda t: jnp.minimum(t, n_tiles - 1)
        prev = lambda t: jnp.maximum(t - 1, 0)
        tok_cur = pl.BlockSpec((TILE, D_MODEL), lambda t: (cur(t), 0))
        tok_prev = pl.BlockSpec((TILE, D_MODEL), lambda t: (prev(t), 0))
        kv_out = pl.BlockSpec((1, WINDOW, KV_WIDTH), lambda t: (cur(t) // tiles_per_seq, 0, 0))
        in_specs = [tok_cur, tok_prev] + common_in[1:]
        out_specs = [tok_prev, kv_out, kv_out]
        out_shape = [jax.ShapeDtypeStruct((tokens, D_MODEL), F32),
                     jax.ShapeDtypeStruct((n_batch, WINDOW, KV_WIDTH), F32),
                     jax.ShapeDtypeStruct((n_batch, WINDOW, KV_WIDTH), F32)]
        kv_scratch = [pltpu.VMEM((WINDOW + TILE, N_KV_HEADS * LANES), BF16)] * 2
        tail_scratch = tail_scratch + [pltpu.VMEM((TILE, GMLP_WIDTH), BF16),
                                       pltpu.VMEM((TILE, ATTN_WIDTH), BF16)]
        body = functools.partial(_mixer_prompt_kernel, n_steps=n_tiles + 1, tiles_per_seq=tiles_per_seq)
        args, lead, name = (), (x2d, x2d), "mixer_prompt"
    else:
        n_seq = TILE // CHUNK
        grid = (tokens // TILE,)
        tok = pl.BlockSpec((TILE, D_MODEL), lambda t: (t, 0))
        cache_spec = pl.BlockSpec((n_seq, WINDOW, KV_WIDTH), lambda t: (t, 0, 0))
        kv_out = pl.BlockSpec((n_seq, CHUNK, KV_WIDTH), lambda t: (t, 0, 0))
        in_specs = [tok] + common_in[1:] + [cache_spec, cache_spec]
        out_specs = [tok, kv_out, kv_out, pl.BlockSpec((TILE, GMLP_WIDTH), lambda t: (t, 0))]
        out_shape = [jax.ShapeDtypeStruct((tokens, D_MODEL), F32),
                     jax.ShapeDtypeStruct((n_batch, CHUNK, KV_WIDTH), F32),
                     jax.ShapeDtypeStruct((n_batch, CHUNK, KV_WIDTH), F32),
                     jax.ShapeDtypeStruct((tokens, GMLP_WIDTH), F32)]
        kv_scratch = [pltpu.VMEM((n_seq, KEYS, N_KV_HEADS * LANES), BF16)] * 2
        body, args, lead, name = _mixer_sample_kernel, tuple(caches), (x2d,), "mixer_sample"
    return pl.pallas_call(
        body,
        grid=grid,
        in_specs=in_specs,
        out_specs=out_specs,
        out_shape=out_shape,
        scratch_shapes=common_scratch + kv_scratch + tail_scratch,
        compiler_params=pltpu.CompilerParams(
            dimension_semantics=("arbitrary",) * len(grid), vmem_limit_bytes=VMEM_LIMIT_BYTES),
        name=name,
    )(*lead, lng, win, lvg, lvb, ws, bsb, sinks, bias, wpa, wpb, wo, *args)


def _ffn(x2d, n_batch, params, cache_conv):
    lnf, wup, wconv, bconv, wdown, lnfin = params
    tokens = x2d.shape[0]
    sample = cache_conv is not None
    weights_in = [_resident(lnf.shape), _resident(wup.shape), _resident(wconv.shape), _resident(bconv.shape),
                  _resident(wdown.shape), _resident(lnfin.shape)]
    scratch = [pltpu.VMEM((FFN_TILE, D_MODEL), BF16),
               pltpu.VMEM((FFN_TILE, D_FF), BF16)]
    if not sample:
        n_tiles = tokens // n_batch // FFN_TILE
        grid = (n_batch, n_tiles)
        tok = pl.BlockSpec((FFN_TILE, D_MODEL), lambda b, s: (b * n_tiles + s, 0))
        in_specs = [tok] + weights_in
        conv_out = pl.BlockSpec((1, CONV_W - 1, D_FF), lambda b, s: (b, 0, 0))
        scratch += [pltpu.VMEM((1, SUBLANES + FFN_TILE, FF_CHUNK), F32),
                    pltpu.VMEM((SUBLANES, D_FF), F32)]
        args, name = (), "ffn_prompt"
    else:
        n_seq = FFN_TILE // CHUNK
        grid = (tokens // FFN_TILE,)
        tok = pl.BlockSpec((FFN_TILE, D_MODEL), lambda t: (t, 0))
        conv_out = pl.BlockSpec((n_seq, CONV_W - 1, D_FF), lambda t: (t, 0, 0))
        in_specs = [tok] + weights_in + [conv_out]
        scratch += [pltpu.VMEM((n_seq, SUBLANES + CHUNK, FF_CHUNK), F32)]
        args, name = (cache_conv,), "ffn_sample"
    return pl.pallas_call(
        functools.partial(_ffn_kernel, sample=sample),
        grid=grid,
        in_specs=in_specs,
        out_specs=[tok, conv_out],
        out_shape=[jax.ShapeDtypeStruct((tokens, D_MODEL), F32),
                   jax.ShapeDtypeStruct((n_batch, CONV_W - 1, D_FF), F32)],
        scratch_shapes=scratch,
        compiler_params=pltpu.CompilerParams(
            dimension_semantics=("arbitrary",) * len(grid), vmem_limit_bytes=VMEM_LIMIT_BYTES),
        name=name,
    )(x2d, lnf, wup, wconv, bconv, wdown, lnfin, *args)


def kernel(x_prompt, x_sample, cache_k, cache_v, cache_conv, ln_mix_g, w_in, ln_v_g, ln_v_b, w_s, b_s,
           attn_sinks, w_pa, w_pb, w_o, ln_ffn_g, w_up, w_conv, b_conv, w_down, ln_final_g):
    depth = w_in.shape[0]
    assert depth == 1, "single-layer kernel"
    n_prompt, seq, _ = x_prompt.shape
    n_sample, dec_seq, _ = x_sample.shape
    assert seq % FFN_TILE == 0 and seq % TILE == 0 and dec_seq == CHUNK
    assert (n_sample * dec_seq) % FFN_TILE == 0 and (n_sample * dec_seq) % TILE == 0
    assert cache_k.shape[2] == WINDOW

    row = lambda p: p.reshape(1, -1)
    mixer_params = (
        row(ln_mix_g[0]), w_in[0].astype(BF16), row(ln_v_g[0]), row(ln_v_b[0]),
        w_s[0], jnp.broadcast_to(b_s[0][:, :, None], (GMLP_GROUPS, GMLP_CHUNK, LANES)),
        attn_sinks[0], _alibi_bias(),
        w_pa[0].astype(BF16), w_pb[0].astype(BF16), w_o[0].astype(BF16))
    ffn_params = (row(ln_ffn_g[0]), w_up[0].astype(BF16), w_conv[0], row(b_conv[0]),
                  w_down[0].astype(BF16), row(ln_final_g))

    xp = x_prompt.reshape(n_prompt * seq, D_MODEL)
    x1p, kp, vp = _mixer(xp, n_prompt, mixer_params, None)
    yp, convp = _ffn(x1p, n_prompt, ffn_params, None)

    xs = x_sample.reshape(n_sample * dec_seq, D_MODEL)
    caches = (cache_k[0].reshape(n_sample, WINDOW, KV_WIDTH), cache_v[0].reshape(n_sample, WINDOW, KV_WIDTH))
    x1s, ks, vs, gvs = _mixer(xs, n_sample, mixer_params, caches)
    ys, convs = _ffn(x1s, n_sample, ffn_params, cache_conv[0])

    kv_shape = lambda a: a.reshape(1, a.shape[0], a.shape[1], N_KV_HEADS, HEAD_DIM)
    return (yp.reshape(n_prompt, seq, D_MODEL), ys.reshape(n_sample, dec_seq, D_MODEL),
            kv_shape(kp), kv_shape(vp), convp[None],
            kv_shape(ks), kv_shape(vs), convs[None],
            gvs.reshape(1, n_sample, dec_seq, GMLP_WIDTH))
```

```python
import functools

import jax
import jax.numpy as jnp
from jax import lax
from jax.experimental import pallas as pl
from jax.experimental.pallas import tpu as pltpu

F32 = jnp.float32
BF16 = jnp.bfloat16

D_MODEL = 1024
GMLP_WIDTH = 1024
GMLP_GROUPS = 8
GMLP_CHUNK = 128
N_HEADS = 16
N_KV_HEADS = 4
GQA_GROUP = N_HEADS // N_KV_HEADS
HEAD_DIM = 64
ATTN_WIDTH = N_HEADS * HEAD_DIM
KV_WIDTH = N_KV_HEADS * HEAD_DIM
CHUNK = 64
WINDOW = 128
KEYS = WINDOW + CHUNK
D_FF = 2816
CONV_W = 3
EPS = 1e-6
NEG_INF = -1e30
COL_U = 0
COL_VG = COL_U + GMLP_WIDTH
COL_Q = COL_VG + GMLP_WIDTH
COL_K = COL_Q + ATTN_WIDTH
COL_V = COL_K + KV_WIDTH
COL_GATE = COL_V + KV_WIDTH
IN_COLS = COL_GATE + 2 * D_MODEL

LANES = 128
SUBLANES = 8
HALF = LANES // 2

TILE = 512
FF_CHUNK = 256
UNIT = 256
PROMPT_SLOTS = (
    (("m", 0), ("m", 1)), (("m", 2), ("m", 3), ("o", 0)),
    (("o", 1), ("o", 2), ("o", 3), ("g", 0), ("g", 1)),
    (("g", 2), ("g", 3), ("g", 4), ("g", 5), ("g", 6), ("g", 7)),
)
QBLOCK = 2 * CHUNK
KWIN = QBLOCK + WINDOW
ONES_ROWS = 16
FFN_TILE = 1024
FFN_EDGE_ROWS = 256
VMEM_LIMIT_BYTES = 52 * 1024 * 1024
MIXER_VMEM_LIMIT_BYTES = 57 * 1024 * 1024


def _rms_norm(x, g):
    return x * lax.rsqrt(jnp.mean(x * x, axis=-1, keepdims=True) + EPS) * g


def _layer_norm(x, g, b):
    mu = jnp.mean(x, axis=-1, keepdims=True)
    xc = x - mu
    return xc * lax.rsqrt(jnp.mean(xc * xc, axis=-1, keepdims=True) + EPS) * g + b


def _col(x, c):
    return x[:, c * LANES:(c + 1) * LANES]


def _swap_halves(x):
    n = x.shape[1] // LANES
    return jnp.concatenate([pltpu.roll(_col(x, c), HALF, axis=1) for c in range(n)], axis=1)


def _low_half_mask():
    return lax.broadcasted_iota(jnp.int32, (1, LANES), 1) < HALF


def _proj(h_scr, win_ref, c0, n):
    return jnp.dot(h_scr[...], win_ref[:, c0:c0 + n], preferred_element_type=F32)


def _project_gmlp(hn, win_ref, lvg_ref, lvb_ref, h_scr, u_scr, vn_scr):
    h_scr[...] = hn.astype(BF16)
    u_scr[...] = jax.nn.gelu(_proj(h_scr, win_ref, COL_U, GMLP_WIDTH))
    vn = _layer_norm(jax.nn.gelu(_proj(h_scr, win_ref, COL_VG, GMLP_WIDTH)), lvg_ref[...], lvb_ref[...])
    vn_scr[...] = vn.astype(BF16)
    return vn


def _project(x_ref, lng_ref, win_ref, lvg_ref, lvb_ref, h_scr, u_scr, vn_scr, q_scr):
    vn = _project_gmlp(_rms_norm(x_ref[...], lng_ref[...]), win_ref, lvg_ref, lvb_ref, h_scr, u_scr, vn_scr)

    def proj(c0, n):
        return _proj(h_scr, win_ref, c0, n)

    q = proj(COL_Q, ATTN_WIDTH) * (HEAD_DIM ** -0.5)
    qs = _swap_halves(q)
    blocks = []
    for c in range(ATTN_WIDTH // LANES):
        blocks += [_col(q, c), _col(qs, c)]
    q_scr[...] = jnp.concatenate(blocks, axis=1).astype(BF16)
    kv = proj(COL_K, 2 * KV_WIDTH)
    return kv[:, :KV_WIDTH], kv[:, KV_WIDTH:], vn


def _kv_blocks(k, v):
    lo = _low_half_mask()
    ks, vs = _swap_halves(k), _swap_halves(v)
    kb, vb = [], []
    for j in range(N_KV_HEADS):
        c = j // 2
        if j % 2 == 0:
            kb.append(jnp.where(lo, _col(k, c), 0.0))
            vb.append(jnp.where(lo, _col(v, c), _col(vs, c)))
        else:
            kb.append(jnp.where(lo, _col(ks, c), 0.0))
            vb.append(jnp.where(lo, _col(vs, c), _col(v, c)))
    return jnp.concatenate(kb, axis=1).astype(BF16), jnp.concatenate(vb, axis=1).astype(BF16)


def _k_blocks(k):
    lo = _low_half_mask()
    ks = _swap_halves(k)
    kb = [jnp.where(lo, _col(k if j % 2 == 0 else ks, j // 2), 0.0) for j in range(N_KV_HEADS)]
    return jnp.concatenate(kb, axis=1).astype(BF16)


def _scores_t(r0, qt_scr, k_scr):
    out = []
    for j in range(N_KV_HEADS):
        rhs = jnp.concatenate(
            [qt_scr[(GQA_GROUP * j + g) * HEAD_DIM:(GQA_GROUP * j + g) * HEAD_DIM + LANES, r0:r0 + QBLOCK]
             for g in range(GQA_GROUP)], axis=1)
        out.append(jnp.dot(k_scr[r0:r0 + KWIN, j * LANES:(j + 1) * LANES], rhs, preferred_element_type=F32))
    return out


def _softmax_pv_t(r0, scores, vt_scr, sink_ref, bias_ref, first_valid_row, bt_scr):
    ones = jnp.ones((ONES_ROWS, KWIN), BF16)
    for j in range(N_KV_HEADS):
        s = scores[j] - bias_ref[j]
        if first_valid_row is not None:
            ri = lax.broadcasted_iota(jnp.int32, s.shape, 0)
            s = jnp.where(ri >= first_valid_row, s, NEG_INF)
        sink = sink_ref[j]
        m = jnp.maximum(jnp.max(s, axis=0, keepdims=True), sink)
        e = jnp.exp(s - m).astype(BF16)
        lhs = jnp.concatenate([vt_scr[j * HEAD_DIM:(j + 1) * HEAD_DIM, r0:r0 + KWIN], ones], axis=0)
        o = jnp.dot(lhs, e, preferred_element_type=F32)
        out = o[:HEAD_DIM] / (o[HEAD_DIM:HEAD_DIM + 1] + jnp.exp(sink - m))
        for g in range(GQA_GROUP):
            h = GQA_GROUP * j + g
            bt_scr[h * HEAD_DIM:(h + 1) * HEAD_DIM, r0:r0 + QBLOCK] = out[:, g * QBLOCK:(g + 1) * QBLOCK]


def _scores(qc, kctx):
    out = []
    for j in range(N_KV_HEADS):
        qs = jnp.concatenate([_col(qc, GQA_GROUP * j + g) for g in range(GQA_GROUP)], axis=0)
        out.append(lax.dot_general(qs, kctx(j), (((1,), (1,)), ((), ())), preferred_element_type=F32))
    return out


def _softmax_pv(scores, vctx, sink_ref, bias_ref, first_valid_key):
    lo = _low_half_mask()
    ones = jnp.ones((KEYS, LANES), BF16)
    outs = []
    for j in range(N_KV_HEADS):
        heads = [GQA_GROUP * j + g for g in range(GQA_GROUP)]
        s = scores[j]
        es, ms = [], []
        for g, h in enumerate(heads):
            sg = s[g * CHUNK:(g + 1) * CHUNK] - bias_ref[h]
            if first_valid_key is not None:
                kj = lax.broadcasted_iota(jnp.int32, (1, KEYS), 1)
                sg = jnp.where(kj >= first_valid_key, sg, NEG_INF)
            m = jnp.maximum(jnp.max(sg, axis=-1, keepdims=True), sink_ref[h])
            es.append(jnp.exp(sg - m).astype(BF16))
            ms.append(m)
        o = jnp.dot(jnp.concatenate(es, axis=0), jnp.concatenate([vctx(j), ones], axis=1),
                    preferred_element_type=F32)
        for p in range(GQA_GROUP // 2):
            g0, g1 = 2 * p, 2 * p + 1
            o0, o1 = o[g0 * CHUNK:(g0 + 1) * CHUNK], o[g1 * CHUNK:(g1 + 1) * CHUNK]
            num = jnp.where(lo, o0[:, :LANES], o1[:, :LANES])
            den = jnp.where(lo, o0[:, LANES:], o1[:, LANES:])
            sink_term = jnp.where(lo, jnp.exp(sink_ref[heads[g0]] - ms[g0]),
                                  jnp.exp(sink_ref[heads[g1]] - ms[g1]))
            outs.append(num / (den + sink_term))
    return outs


def _gate_unit(k, h_scr, win_ref, g_scr):
    cols = slice(k * UNIT, (k + 1) * UNIT)
    g_scr[:, cols] = jax.nn.sigmoid(
        jnp.dot(h_scr[...], win_ref[:, COL_GATE + k * UNIT:COL_GATE + (k + 1) * UNIT], preferred_element_type=F32))


def _merge_unit(i, g_scr, a_scr, b_scr, m_scr, wpa_ref, wpb_ref):
    cols = slice(i * UNIT, (i + 1) * UNIT)
    ya = jnp.dot(a_scr[...], wpa_ref[:, cols], preferred_element_type=F32)
    yb = jnp.dot(b_scr[...], wpb_ref[:, cols], preferred_element_type=F32)
    m_scr[:, cols] = (g_scr[:, cols] * ya + g_scr[:, D_MODEL + i * UNIT:D_MODEL + (i + 1) * UNIT] * yb).astype(BF16)


def _out_unit(i, x_ref, m_scr, wo_ref, x1_ref):
    cols = slice(i * UNIT, (i + 1) * UNIT)
    x1_ref[:, cols] = x_ref[:, cols] + jnp.dot(m_scr[...], wo_ref[:, cols], preferred_element_type=F32)


def _mixer_prompt_kernel(x_ref, xprev_ref, lng_ref, win_ref, lvg_ref, lvb_ref, ws_ref, bsb_ref, sink_ref, bias_ref,
                         wpa_ref, wpb_ref, wo_ref, wqt_ref, wvt_ref,
                         x1_ref, knew_ref, vnew_ref,
                         h_scr, u_scr, vn_scr, ht_scr, qt_scr, k_scr, vt_scr, bt_scr,
                         a_scr, b_scr, m_scr, g_scr, a_new, b_new,
                         *, n_steps, tiles_per_seq):
    t = pl.program_id(0)
    seq_tile = lax.rem(jnp.minimum(t, n_steps - 2), tiles_per_seq)

    @pl.when(t == 0)
    def _():
        a_scr[...] = jnp.zeros(a_scr.shape, BF16)
        b_scr[...] = jnp.zeros(b_scr.shape, BF16)
        g_scr[...] = jnp.zeros(g_scr.shape, F32)
        k_scr[0:WINDOW, :] = jnp.zeros((WINDOW, N_KV_HEADS * LANES), BF16)
        vt_scr[:, 0:WINDOW] = jnp.zeros((KV_WIDTH, WINDOW), BF16)
        qt_scr[ATTN_WIDTH:, :] = jnp.zeros((HEAD_DIM, TILE), BF16)

    def run_unit(kind, i):
        if kind == "m":
            _merge_unit(i, g_scr, a_scr, b_scr, m_scr, wpa_ref, wpb_ref)
        elif kind == "o":
            _out_unit(i, xprev_ref, m_scr, wo_ref, x1_ref)
        else:
            _gate_unit(i, h_scr, win_ref, g_scr)

    hn = _rms_norm(x_ref[...], lng_ref[...])
    ht_scr[...] = hn.T.astype(BF16)
    h_scr[...] = hn.astype(BF16)
    u_scr[...] = jax.nn.gelu(_proj(h_scr, win_ref, COL_U, GMLP_WIDTH))
    vg = _proj(h_scr, win_ref, COL_VG, GMLP_WIDTH)
    qt = jnp.dot(wqt_ref[...], ht_scr[...], preferred_element_type=F32) * (HEAD_DIM ** -0.5)
    vn_scr[...] = _layer_norm(jax.nn.gelu(vg), lvg_ref[...], lvb_ref[...]).astype(BF16)
    qt_scr[0:ATTN_WIDTH, :] = qt.astype(BF16)
    k = _proj(h_scr, win_ref, COL_K, KV_WIDTH)
    knew_ref[0] = k[TILE - WINDOW:]
    k_scr[WINDOW:, :] = _k_blocks(k)
    vt_scr[:, WINDOW:] = jnp.dot(wvt_ref[...], ht_scr[...], preferred_element_type=F32).astype(BF16)
    vnew_ref[0] = jnp.dot(h_scr[TILE - WINDOW:, :], win_ref[:, COL_V:COL_V + KV_WIDTH],
                          preferred_element_type=F32)

    ri = lax.broadcasted_iota(jnp.int32, (GMLP_CHUNK, GMLP_CHUNK), 0)
    ci = lax.broadcasted_iota(jnp.int32, (GMLP_CHUNK, GMLP_CHUNK), 1)
    causal = jnp.where((ci < CHUNK) | (ri >= CHUNK), 1.0, 0.0).astype(F32)
    n_blk = TILE // GMLP_CHUNK
    for g in range(GMLP_GROUPS):
        wm = (ws_ref[g] * causal).astype(BF16)
        vcat = jnp.concatenate(
            [vn_scr[n * GMLP_CHUNK:(n + 1) * GMLP_CHUNK, g * LANES:(g + 1) * LANES] for n in range(n_blk)], axis=1)
        sp = jnp.dot(wm, vcat, preferred_element_type=F32)
        for n in range(n_blk):
            rows = slice(n * GMLP_CHUNK, (n + 1) * GMLP_CHUNK)
            cols = slice(g * LANES, (g + 1) * LANES)
            a_new[rows, cols] = (u_scr[rows, cols] * (_col(sp, n) + bsb_ref[g])).astype(BF16)

    n_blocks = TILE // QBLOCK
    scores = _scores_t(0, qt_scr, k_scr)
    for blk in range(n_blocks):
        r0 = blk * QBLOCK
        first_valid = jnp.where(seq_tile == 0, WINDOW, 0) if blk == 0 else None
        next_scores = _scores_t(r0 + QBLOCK, qt_scr, k_scr) if blk + 1 < n_blocks else None
        for kind, i in PROMPT_SLOTS[blk]:
            run_unit(kind, i)
        _softmax_pv_t(r0, scores, vt_scr, sink_ref, bias_ref, first_valid, bt_scr)
        scores = next_scores
        b_new[r0:r0 + QBLOCK, :] = bt_scr[:, r0:r0 + QBLOCK].T.astype(BF16)
    k_scr[0:WINDOW, :] = k_scr[TILE:TILE + WINDOW, :]
    vt_scr[:, 0:WINDOW] = vt_scr[:, TILE:TILE + WINDOW]
    a_scr[...] = a_new[...]
    b_scr[...] = b_new[...]


def _mixer_sample_kernel(x_ref, lng_ref, win_ref, lvg_ref, lvb_ref, ws_ref, bsb_ref, sink_ref, bias_ref,
                         wpa_ref, wpb_ref, wo_ref, ck_ref, cv_ref,
                         x1_ref, knew_ref, vnew_ref, gv_ref,
                         h_scr, u_scr, vn_scr, q_scr, k_scr, v_scr, a_scr, b_scr, m_scr, g_scr):
    n_seq = TILE // CHUNK
    k, v, vn = _project(x_ref, lng_ref, win_ref, lvg_ref, lvb_ref, h_scr, u_scr, vn_scr, q_scr)
    gv_ref[...] = vn
    for c in range(n_seq):
        knew_ref[c] = k[c * CHUNK:(c + 1) * CHUNK]
        vnew_ref[c] = v[c * CHUNK:(c + 1) * CHUNK]

    for g in range(GMLP_GROUPS):
        wm = ws_ref[g][:CHUNK, :CHUNK].astype(BF16)
        vcat = jnp.concatenate(
            [vn_scr[c * CHUNK:(c + 1) * CHUNK, g * LANES:(g + 1) * LANES] for c in range(n_seq)], axis=1)
        sp = jnp.dot(wm, vcat, preferred_element_type=F32)
        bs = bsb_ref[g][:CHUNK]
        for c in range(n_seq):
            rows = slice(c * CHUNK, (c + 1) * CHUNK)
            cols = slice(g * LANES, (g + 1) * LANES)
            a_scr[rows, cols] = (u_scr[rows, cols] * (_col(sp, c) + bs)).astype(BF16)

    kb, vb = _kv_blocks(k, v)
    for c in range(n_seq):
        ckb, cvb = _kv_blocks(ck_ref[c], cv_ref[c])
        k_scr[c, 0:WINDOW, :] = ckb
        v_scr[c, 0:WINDOW, :] = cvb
        k_scr[c, WINDOW:, :] = kb[c * CHUNK:(c + 1) * CHUNK]
        v_scr[c, WINDOW:, :] = vb[c * CHUNK:(c + 1) * CHUNK]
    for c in range(n_seq):
        r0 = c * CHUNK
        scores = _scores(q_scr[r0:r0 + CHUNK, :], lambda j: k_scr[c, :, j * LANES:(j + 1) * LANES])
        _gate_unit(c, h_scr, win_ref, g_scr)
        outs = _softmax_pv(scores, lambda j: v_scr[c, :, j * LANES:(j + 1) * LANES], sink_ref, bias_ref, None)
        b_scr[r0:r0 + CHUNK, :] = jnp.concatenate(outs, axis=1).astype(BF16)

    assert n_seq == 2 * D_MODEL // UNIT
    for i in range(D_MODEL // UNIT):
        _merge_unit(i, g_scr, a_scr, b_scr, m_scr, wpa_ref, wpb_ref)
    for i in range(D_MODEL // UNIT):
        _out_unit(i, x_ref, m_scr, wo_ref, x1_ref)


def _up_dots(h2_scr, wup_ref, c0):
    a = jnp.dot(h2_scr[...], wup_ref[:, c0:c0 + FF_CHUNK], preferred_element_type=F32)
    val = jnp.dot(h2_scr[...], wup_ref[:, D_FF + c0:D_FF + c0 + FF_CHUNK], preferred_element_type=F32)
    return a, val


def _conv_taps(a_s, buf, w):
    n = a_s.shape[0]
    return (w[0:1] * buf[SUBLANES - 2:SUBLANES - 2 + n, :] + w[1:2] * buf[SUBLANES - 1:SUBLANES - 1 + n, :]) \
        + w[2:3] * a_s


def _ffn_kernel(*refs, sample):
    if sample:
        (x1_ref, lnf_ref, wup_ref, wconv_ref, bconv_ref, wdown_ref, lnfin_ref, cconv_ref,
         y_ref, cnew_ref, h2_scr, g_scr, a_buf) = refs
        seq_len = CHUNK
    else:
        (x1_ref, lnf_ref, wup_ref, wconv_ref, bconv_ref, wdown_ref, lnfin_ref,
         y_ref, cnew_ref, h2_scr, g_scr, a_buf, hist_scr) = refs
        seq_len = FFN_TILE

        @pl.when(pl.program_id(1) == 0)
        def _():
            hist_scr[...] = jnp.zeros(hist_scr.shape, F32)

    n_seq = FFN_TILE // seq_len
    row_blocks = [slice(r, r + FFN_EDGE_ROWS) for r in range(0, FFN_TILE, FFN_EDGE_ROWS)]
    for rows in row_blocks:
        h2_scr[rows, :] = _rms_norm(x1_ref[rows, :], lnf_ref[...]).astype(BF16)
    for j, c0 in enumerate(range(0, D_FF, FF_CHUNK)):
        cols = slice(c0, c0 + FF_CHUNK)
        if j == 0:
            parts = [_up_dots(h2_scr.at[rows, :], wup_ref, c0) for rows in row_blocks]
            a = jnp.concatenate([p[0] for p in parts], axis=0)
            val = jnp.concatenate([p[1] for p in parts], axis=0)
        else:
            a, val = _up_dots(h2_scr, wup_ref, c0)
        w = wconv_ref[:, cols]
        taps = []
        for s in range(n_seq):
            a_s = a[s * seq_len:(s + 1) * seq_len]
            buf = a_buf.at[s]
            if sample:
                buf[SUBLANES - (CONV_W - 1):SUBLANES, :] = cconv_ref[s, :, cols]
            else:
                buf[0:SUBLANES, :] = hist_scr[:, cols]
                hist_scr[:, cols] = a_s[seq_len - SUBLANES:]
            buf[SUBLANES:, :] = a_s
            cnew_ref[s, :, cols] = a_s[seq_len - (CONV_W - 1):]
            taps.append(_conv_taps(a_s, buf, w))
        conv = bconv_ref[:, cols] + (taps[0] if n_seq == 1 else jnp.concatenate(taps, axis=0))
        g_scr[:, cols] = (jax.nn.gelu(conv) * val).astype(BF16)
    for rows in row_blocks:
        x2 = x1_ref[rows, :] + jnp.dot(g_scr[rows, :], wdown_ref[...], preferred_element_type=F32)
        y_ref[rows, :] = _rms_norm(x2, lnfin_ref[...])


def _resident(shape):
    nd = len(shape)
    return pl.BlockSpec(shape, lambda *_: (0,) * nd, pipeline_mode=pl.Buffered(1))


def _alibi_bias():
    slopes = jnp.exp2(-8.0 * jnp.arange(1, N_HEADS + 1, dtype=F32) / N_HEADS)
    qi = jnp.arange(CHUNK)[:, None]
    kj = jnp.arange(KEYS)[None, :]
    dist = jnp.abs(qi + WINDOW - kj).astype(F32)
    return slopes[:, None, None] * dist[None]


def _alibi_bias_t():
    slopes = jnp.exp2(-8.0 * jnp.arange(1, N_HEADS + 1, dtype=F32) / N_HEADS).reshape(N_KV_HEADS, 1, GQA_GROUP, 1)
    kr = jnp.arange(KWIN)[:, None]
    q = jnp.arange(QBLOCK)[None, :]
    dist = jnp.abs(q + WINDOW - kr).astype(F32)
    in_band = (kr // CHUNK >= q // CHUNK) & (kr // CHUNK <= q // CHUNK + WINDOW // CHUNK)
    bias = jnp.where(in_band[None, :, None, :], slopes * dist[None, :, None, :], -NEG_INF)
    return bias.reshape(N_KV_HEADS, KWIN, GQA_GROUP * QBLOCK)


def _mixer(x2d, n_batch, params, attn, caches):
    lng, win, lvg, lvb, ws, bsb, wpa, wpb, wo = params
    tokens = x2d.shape[0]
    sample = caches is not None
    common_in = [
        None,
        _resident(lng.shape), _resident(win.shape), _resident(lvg.shape), _resident(lvb.shape),
        _resident(ws.shape), _resident(bsb.shape),
        pl.BlockSpec(memory_space=pltpu.SMEM) if sample else _resident(attn[0].shape),
        _resident(attn[1].shape), _resident(wpa.shape), _resident(wpb.shape), _resident(wo.shape),
    ]
    common_scratch = [
        pltpu.VMEM((TILE, D_MODEL), BF16),
        pltpu.VMEM((TILE, GMLP_WIDTH), F32),
        pltpu.VMEM((TILE, GMLP_WIDTH), BF16),
    ]
    tail_scratch = [
        pltpu.VMEM((TILE, GMLP_WIDTH), BF16),
        pltpu.VMEM((TILE, ATTN_WIDTH), BF16),
        pltpu.VMEM((TILE, D_MODEL), BF16),
        pltpu.VMEM((TILE, 2 * D_MODEL), F32),
    ]
    if not sample:
        tiles_per_seq = tokens // n_batch // TILE
        n_tiles = n_batch * tiles_per_seq
        grid = (n_tiles + 1,)
        cur = lambda t: jnp.minimum(t, n_tiles - 1)
        prev = lambda t: jnp.maximum(t - 1, 0)
        tok_cur = pl.BlockSpec((TILE, D_MODEL), lambda t: (cur(t), 0))
        tok_prev = pl.BlockSpec((TILE, D_MODEL), lambda t: (prev(t), 0))
        kv_out = pl.BlockSpec((1, WINDOW, KV_WIDTH), lambda t: (cur(t) // tiles_per_seq, 0, 0))
        in_specs = [tok_cur, tok_prev] + common_in[1:] + [_resident(attn[2].shape), _resident(attn[3].shape)]
        out_specs = [tok_prev, kv_out, kv_out]
        out_shape = [jax.ShapeDtypeStruct((tokens, D_MODEL), F32),
                     jax.ShapeDtypeStruct((n_batch, WINDOW, KV_WIDTH), F32),
                     jax.ShapeDtypeStruct((n_batch, WINDOW, KV_WIDTH), F32)]
        kv_scratch = [pltpu.VMEM((D_MODEL, TILE), BF16),
                      pltpu.VMEM((ATTN_WIDTH + HEAD_DIM, TILE), BF16),
                      pltpu.VMEM((WINDOW + TILE, N_KV_HEADS * LANES), BF16),
                      pltpu.VMEM((KV_WIDTH, WINDOW + TILE), BF16),
                      pltpu.VMEM((ATTN_WIDTH, TILE), F32)]
        tail_scratch = tail_scratch + [pltpu.VMEM((TILE, GMLP_WIDTH), BF16),
                                       pltpu.VMEM((TILE, ATTN_WIDTH), BF16)]
        body = functools.partial(_mixer_prompt_kernel, n_steps=n_tiles + 1, tiles_per_seq=tiles_per_seq)
        args, lead, name = (attn[2], attn[3]), (x2d, x2d), "mixer_prompt"
    else:
        n_seq = TILE // CHUNK
        grid = (tokens // TILE,)
        tok = pl.BlockSpec((TILE, D_MODEL), lambda t: (t, 0))
        cache_spec = pl.BlockSpec((n_seq, WINDOW, KV_WIDTH), lambda t: (t, 0, 0))
        kv_out = pl.BlockSpec((n_seq, CHUNK, KV_WIDTH), lambda t: (t, 0, 0))
        in_specs = [tok] + common_in[1:] + [cache_spec, cache_spec]
        out_specs = [tok, kv_out, kv_out, pl.BlockSpec((TILE, GMLP_WIDTH), lambda t: (t, 0))]
        out_shape = [jax.ShapeDtypeStruct((tokens, D_MODEL), F32),
                     jax.ShapeDtypeStruct((n_batch, CHUNK, KV_WIDTH), F32),
                     jax.ShapeDtypeStruct((n_batch, CHUNK, KV_WIDTH), F32),
                     jax.ShapeDtypeStruct((tokens, GMLP_WIDTH), F32)]
        kv_scratch = [pltpu.VMEM((TILE, N_HEADS * LANES), BF16)]
        kv_scratch += [pltpu.VMEM((n_seq, KEYS, N_KV_HEADS * LANES), BF16)] * 2
        body, args, lead, name = _mixer_sample_kernel, tuple(caches), (x2d,), "mixer_sample"
    return pl.pallas_call(
        body,
        grid=grid,
        in_specs=in_specs,
        out_specs=out_specs,
        out_shape=out_shape,
        scratch_shapes=common_scratch + kv_scratch + tail_scratch,
        compiler_params=pltpu.CompilerParams(
            dimension_semantics=("arbitrary",) * len(grid), vmem_limit_bytes=MIXER_VMEM_LIMIT_BYTES),
        name=name,
    )(*lead, lng, win, lvg, lvb, ws, bsb, attn[0], attn[1], wpa, wpb, wo, *args)


def _ffn(x2d, n_batch, params, cache_conv):
    lnf, wup, wconv, bconv, wdown, lnfin = params
    tokens = x2d.shape[0]
    sample = cache_conv is not None
    weights_in = [_resident(lnf.shape), _resident(wup.shape), _resident(wconv.shape), _resident(bconv.shape),
                  _resident(wdown.shape), _resident(lnfin.shape)]
    scratch = [pltpu.VMEM((FFN_TILE, D_MODEL), BF16),
               pltpu.VMEM((FFN_TILE, D_FF), BF16)]
    if not sample:
        n_tiles = tokens // n_batch // FFN_TILE
        grid = (n_batch, n_tiles)
        tok = pl.BlockSpec((FFN_TILE, D_MODEL), lambda b, s: (b * n_tiles + s, 0))
        in_specs = [tok] + weights_in
        conv_out = pl.BlockSpec((1, CONV_W - 1, D_FF), lambda b, s: (b, 0, 0))
        scratch += [pltpu.VMEM((1, SUBLANES + FFN_TILE, FF_CHUNK), F32),
                    pltpu.VMEM((SUBLANES, D_FF), F32)]
        args, name = (), "ffn_prompt"
    else:
        n_seq = FFN_TILE // CHUNK
        grid = (tokens // FFN_TILE,)
        tok = pl.BlockSpec((FFN_TILE, D_MODEL), lambda t: (t, 0))
        conv_out = pl.BlockSpec((n_seq, CONV_W - 1, D_FF), lambda t: (t, 0, 0))
        in_specs = [tok] + weights_in + [conv_out]
        scratch += [pltpu.VMEM((n_seq, SUBLANES + CHUNK, FF_CHUNK), F32)]
        args, name = (cache_conv,), "ffn_sample"
    return pl.pallas_call(
        functools.partial(_ffn_kernel, sample=sample),
        grid=grid,
        in_specs=in_specs,
        out_specs=[tok, conv_out],
        out_shape=[jax.ShapeDtypeStruct((tokens, D_MODEL), F32),
                   jax.ShapeDtypeStruct((n_batch, CONV_W - 1, D_FF), F32)],
        scratch_shapes=scratch,
        compiler_params=pltpu.CompilerParams(
            dimension_semantics=("arbitrary",) * len(grid), vmem_limit_bytes=VMEM_LIMIT_BYTES),
        name=name,
    )(x2d, lnf, wup, wconv, bconv, wdown, lnfin, *args)


def kernel(x_prompt, x_sample, cache_k, cache_v, cache_conv, ln_mix_g, w_in, ln_v_g, ln_v_b, w_s, b_s,
           attn_sinks, w_pa, w_pb, w_o, ln_ffn_g, w_up, w_conv, b_conv, w_down, ln_final_g):
    depth = w_in.shape[0]
    assert depth == 1, "single-layer kernel"
    n_prompt, seq, _ = x_prompt.shape
    n_sample, dec_seq, _ = x_sample.shape
    assert seq % FFN_TILE == 0 and seq % TILE == 0 and dec_seq == CHUNK
    assert (n_sample * dec_seq) % FFN_TILE == 0 and (n_sample * dec_seq) % TILE == 0
    assert cache_k.shape[2] == WINDOW

    row = lambda p: p.reshape(1, -1)
    win = w_in[0].astype(BF16)
    mixer_params = (
        row(ln_mix_g[0]), win, row(ln_v_g[0]), row(ln_v_b[0]),
        w_s[0], jnp.broadcast_to(b_s[0][:, :, None], (GMLP_GROUPS, GMLP_CHUNK, LANES)),
        w_pa[0].astype(BF16), w_pb[0].astype(BF16), w_o[0].astype(BF16))
    ffn_params = (row(ln_ffn_g[0]), w_up[0].astype(BF16), w_conv[0], row(b_conv[0]),
                  w_down[0].astype(BF16), row(ln_final_g))
    sinks = attn_sinks[0]
    sink_rows = jnp.repeat(sinks.reshape(N_KV_HEADS, 1, GQA_GROUP), QBLOCK, axis=2)
    prompt_attn = (sink_rows, _alibi_bias_t(),
                   win[:, COL_Q:COL_Q + ATTN_WIDTH].T, win[:, COL_V:COL_V + KV_WIDTH].T)

    xp = x_prompt.reshape(n_prompt * seq, D_MODEL)
    x1p, kp, vp = _mixer(xp, n_prompt, mixer_params, prompt_attn, None)
    yp, convp = _ffn(x1p, n_prompt, ffn_params, None)

    xs = x_sample.reshape(n_sample * dec_seq, D_MODEL)
    caches = (cache_k[0].reshape(n_sample, WINDOW, KV_WIDTH), cache_v[0].reshape(n_sample, WINDOW, KV_WIDTH))
    x1s, ks, vs, gvs = _mixer(xs, n_sample, mixer_params, (sinks, _alibi_bias()), caches)
    ys, convs = _ffn(x1s, n_sample, ffn_params, cache_conv[0])

    kv_shape = lambda a: a.reshape(1, a.shape[0], a.shape[1], N_KV_HEADS, HEAD_DIM)
    return (yp.reshape(n_prompt, seq, D_MODEL), ys.reshape(n_sample, dec_seq, D_MODEL),
            kv_shape(kp), kv_shape(vp), convp[None],
            kv_shape(ks), kv_shape(vs), convs[None],
            gvs.reshape(1, n_sample, dec_seq, GMLP_WIDTH))
```

```python
import functools

import jax
import jax.numpy as jnp
from jax import lax
from jax.experimental import pallas as pl
from jax.experimental.pallas import tpu as pltpu

F32 = jnp.float32
BF16 = jnp.bfloat16

D_MODEL = 1024
GMLP_WIDTH = 1024
GMLP_GROUPS = 8
GMLP_CHUNK = 128
N_HEADS = 16
N_KV_HEADS = 4
GQA_GROUP = N_HEADS // N_KV_HEADS
HEAD_DIM = 64
ATTN_WIDTH = N_HEADS * HEAD_DIM
KV_WIDTH = N_KV_HEADS * HEAD_DIM
CHUNK = 64
WINDOW = 128
KEYS = WINDOW + CHUNK
D_FF = 2816
CONV_W = 3
EPS = 1e-6
NEG_INF = -1e30
COL_U = 0
COL_VG = COL_U + GMLP_WIDTH
COL_Q = COL_VG + GMLP_WIDTH
COL_K = COL_Q + ATTN_WIDTH
COL_V = COL_K + KV_WIDTH
COL_GATE = COL_V + KV_WIDTH
IN_COLS = COL_GATE + 2 * D_MODEL

LANES = 128
SUBLANES = 8
HALF = LANES // 2

TILE = 512
FF_CHUNK = 256
UNIT = 256
PROMPT_SLOTS = (
    (("m", 0),), (("m", 1),), (("m", 2),), (("m", 3),),
    (("o", 0), ("o", 1), ("g", 0)), (("o", 2), ("o", 3), ("g", 1)),
    (("g", 2), ("g", 3), ("g", 4)), (("g", 5), ("g", 6), ("g", 7)),
)
FFN_TILE = 1024
FFN_EDGE_ROWS = 256
VMEM_LIMIT_BYTES = 52 * 1024 * 1024


def _rms_norm(x, g):
    return x * lax.rsqrt(jnp.mean(x * x, axis=-1, keepdims=True) + EPS) * g


def _layer_norm(x, g, b):
    mu = jnp.mean(x, axis=-1, keepdims=True)
    xc = x - mu
    return xc * lax.rsqrt(jnp.mean(xc * xc, axis=-1, keepdims=True) + EPS) * g + b


def _col(x, c):
    return x[:, c * LANES:(c + 1) * LANES]


def _swap_halves(x):
    n = x.shape[1] // LANES
    return jnp.concatenate([pltpu.roll(_col(x, c), HALF, axis=1) for c in range(n)], axis=1)


def _low_half_mask():
    return lax.broadcasted_iota(jnp.int32, (1, LANES), 1) < HALF


def _project(x_ref, lng_ref, win_ref, lvg_ref, lvb_ref, h_scr, u_scr, vn_scr, q_scr):
    h_scr[...] = _rms_norm(x_ref[...], lng_ref[...]).astype(BF16)

    def proj(c0, n):
        return jnp.dot(h_scr[...], win_ref[:, c0:c0 + n], preferred_element_type=F32)

    u_scr[...] = jax.nn.gelu(proj(COL_U, GMLP_WIDTH))
    vn = _layer_norm(jax.nn.gelu(proj(COL_VG, GMLP_WIDTH)), lvg_ref[...], lvb_ref[...])
    vn_scr[...] = vn.astype(BF16)
    q = proj(COL_Q, ATTN_WIDTH) * (HEAD_DIM ** -0.5)
    qs = _swap_halves(q)
    blocks = []
    for c in range(ATTN_WIDTH // LANES):
        blocks += [_col(q, c), _col(qs, c)]
    q_scr[...] = jnp.concatenate(blocks, axis=1).astype(BF16)
    kv = proj(COL_K, 2 * KV_WIDTH)
    return kv[:, :KV_WIDTH], kv[:, KV_WIDTH:], vn


def _kv_blocks(k, v):
    lo = _low_half_mask()
    ks, vs = _swap_halves(k), _swap_halves(v)
    kb, vb = [], []
    for j in range(N_KV_HEADS):
        c = j // 2
        if j % 2 == 0:
            kb.append(jnp.where(lo, _col(k, c), 0.0))
            vb.append(jnp.where(lo, _col(v, c), _col(vs, c)))
        else:
            kb.append(jnp.where(lo, _col(ks, c), 0.0))
            vb.append(jnp.where(lo, _col(vs, c), _col(v, c)))
    return jnp.concatenate(kb, axis=1).astype(BF16), jnp.concatenate(vb, axis=1).astype(BF16)


def _scores(qc, kctx):
    out = []
    for j in range(N_KV_HEADS):
        qs = jnp.concatenate([_col(qc, GQA_GROUP * j + g) for g in range(GQA_GROUP)], axis=0)
        out.append(lax.dot_general(qs, kctx(j), (((1,), (1,)), ((), ())), preferred_element_type=F32))
    return out


def _softmax_pv(scores, vctx, sink_ref, bias_ref, first_valid_key):
    lo = _low_half_mask()
    ones = jnp.ones((KEYS, LANES), BF16)
    outs = []
    for j in range(N_KV_HEADS):
        heads = [GQA_GROUP * j + g for g in range(GQA_GROUP)]
        s = scores[j]
        es, ms = [], []
        for g, h in enumerate(heads):
            sg = s[g * CHUNK:(g + 1) * CHUNK] - bias_ref[h]
            if first_valid_key is not None:
                kj = lax.broadcasted_iota(jnp.int32, (1, KEYS), 1)
                sg = jnp.where(kj >= first_valid_key, sg, NEG_INF)
            m = jnp.maximum(jnp.max(sg, axis=-1, keepdims=True), sink_ref[h])
            es.append(jnp.exp(sg - m).astype(BF16))
            ms.append(m)
        o = jnp.dot(jnp.concatenate(es, axis=0), jnp.concatenate([vctx(j), ones], axis=1),
                    preferred_element_type=F32)
        for p in range(GQA_GROUP // 2):
            g0, g1 = 2 * p, 2 * p + 1
            o0, o1 = o[g0 * CHUNK:(g0 + 1) * CHUNK], o[g1 * CHUNK:(g1 + 1) * CHUNK]
            num = jnp.where(lo, o0[:, :LANES], o1[:, :LANES])
            den = jnp.where(lo, o0[:, LANES:], o1[:, LANES:])
            sink_term = jnp.where(lo, jnp.exp(sink_ref[heads[g0]] - ms[g0]),
                                  jnp.exp(sink_ref[heads[g1]] - ms[g1]))
            outs.append(num / (den + sink_term))
    return outs


def _gmlp_group(g, n_rows, ws_ref, bsb_ref, mask, u_scr, vn_scr, a_out):
    n_blk = TILE // n_rows
    w = ws_ref[g][:n_rows, :n_rows]
    wm = (w if mask is None else w * mask).astype(BF16)
    cols = slice(g * LANES, (g + 1) * LANES)
    vcat = jnp.concatenate([vn_scr[n * n_rows:(n + 1) * n_rows, cols] for n in range(n_blk)], axis=1)
    sp = jnp.dot(wm, vcat, preferred_element_type=F32)
    bs = bsb_ref[g][:n_rows]
    for n in range(n_blk):
        rows = slice(n * n_rows, (n + 1) * n_rows)
        a_out[rows, cols] = (u_scr[rows, cols] * (_col(sp, n) + bs)).astype(BF16)


def _gate_unit(k, h_scr, win_ref, g_scr):
    cols = slice(k * UNIT, (k + 1) * UNIT)
    g_scr[:, cols] = jax.nn.sigmoid(
        jnp.dot(h_scr[...], win_ref[:, COL_GATE + k * UNIT:COL_GATE + (k + 1) * UNIT], preferred_element_type=F32))


def _merge_unit(i, g_scr, a_scr, b_scr, m_scr, wpa_ref, wpb_ref):
    cols = slice(i * UNIT, (i + 1) * UNIT)
    ya = jnp.dot(a_scr[...], wpa_ref[:, cols], preferred_element_type=F32)
    yb = jnp.dot(b_scr[...], wpb_ref[:, cols], preferred_element_type=F32)
    m_scr[:, cols] = (g_scr[:, cols] * ya + g_scr[:, D_MODEL + i * UNIT:D_MODEL + (i + 1) * UNIT] * yb).astype(BF16)


def _out_unit(i, x_ref, m_scr, wo_ref, x1_ref):
    cols = slice(i * UNIT, (i + 1) * UNIT)
    x1_ref[:, cols] = x_ref[:, cols] + jnp.dot(m_scr[...], wo_ref[:, cols], preferred_element_type=F32)


def _mixer_prompt_kernel(x_ref, xprev_ref, lng_ref, win_ref, lvg_ref, lvb_ref, ws_ref, bsb_ref, sink_ref, bias_ref,
                         wpa_ref, wpb_ref, wo_ref,
                         x1_ref, knew_ref, vnew_ref,
                         h_scr, u_scr, vn_scr, q_scr, k_scr, v_scr, a_scr, b_scr, m_scr, g_scr, b_new,
                         *, n_steps, tiles_per_seq):
    t = pl.program_id(0)
    seq_tile = lax.rem(jnp.minimum(t, n_steps - 2), tiles_per_seq)

    @pl.when(t == 0)
    def _():
        a_scr[...] = jnp.zeros(a_scr.shape, BF16)
        b_scr[...] = jnp.zeros(b_scr.shape, BF16)
        g_scr[...] = jnp.zeros(g_scr.shape, F32)
        k_scr[0:WINDOW, :] = jnp.zeros((WINDOW, N_KV_HEADS * LANES), BF16)
        v_scr[0:WINDOW, :] = jnp.zeros((WINDOW, N_KV_HEADS * LANES), BF16)

    def run_unit(kind, i):
        if kind == "m":
            _merge_unit(i, g_scr, a_scr, b_scr, m_scr, wpa_ref, wpb_ref)
        elif kind == "o":
            _out_unit(i, xprev_ref, m_scr, wo_ref, x1_ref)
        else:
            _gate_unit(i, h_scr, win_ref, g_scr)

    k, v, _ = _project(x_ref, lng_ref, win_ref, lvg_ref, lvb_ref, h_scr, u_scr, vn_scr, q_scr)
    knew_ref[0] = k[TILE - WINDOW:]
    vnew_ref[0] = v[TILE - WINDOW:]

    ri = lax.broadcasted_iota(jnp.int32, (GMLP_CHUNK, GMLP_CHUNK), 0)
    ci = lax.broadcasted_iota(jnp.int32, (GMLP_CHUNK, GMLP_CHUNK), 1)
    causal = jnp.where((ci < CHUNK) | (ri >= CHUNK), 1.0, 0.0).astype(F32)

    kb, vb = _kv_blocks(k, v)
    k_scr[WINDOW:, :] = kb
    v_scr[WINDOW:, :] = vb
    n_chunks = TILE // CHUNK
    merge_slots = sum(1 for slot in PROMPT_SLOTS if any(kind == "m" for kind, _ in slot))
    assert all(kind != "m" for slot in PROMPT_SLOTS[merge_slots:] for kind, _ in slot)
    for c in range(n_chunks):
        r0 = c * CHUNK
        first_valid = None
        if c * CHUNK < WINDOW:
            first_valid = jnp.where(seq_tile == 0, WINDOW - c * CHUNK, 0)
        scores = _scores(q_scr[r0:r0 + CHUNK, :], lambda j: k_scr[r0:r0 + KEYS, j * LANES:(j + 1) * LANES])
        for kind, i in PROMPT_SLOTS[c]:
            run_unit(kind, i)
        outs = _softmax_pv(scores, lambda j: v_scr[r0:r0 + KEYS, j * LANES:(j + 1) * LANES],
                           sink_ref, bias_ref, first_valid)
        out = jnp.concatenate(outs, axis=1).astype(BF16)
        if c < merge_slots:
            b_new[r0:r0 + CHUNK, :] = out
        else:
            b_scr[r0:r0 + CHUNK, :] = out
        if c >= merge_slots:
            for g in range((c - merge_slots) * GMLP_GROUPS // (n_chunks - merge_slots),
                           (c - merge_slots + 1) * GMLP_GROUPS // (n_chunks - merge_slots)):
                _gmlp_group(g, GMLP_CHUNK, ws_ref, bsb_ref, causal, u_scr, vn_scr, a_scr)
    k_scr[0:WINDOW, :] = k_scr[TILE:TILE + WINDOW, :]
    v_scr[0:WINDOW, :] = v_scr[TILE:TILE + WINDOW, :]
    b_scr[0:merge_slots * CHUNK, :] = b_new[...]


def _mixer_sample_kernel(x_ref, lng_ref, win_ref, lvg_ref, lvb_ref, ws_ref, bsb_ref, sink_ref, bias_ref,
                         wpa_ref, wpb_ref, wo_ref, ck_ref, cv_ref,
                         x1_ref, knew_ref, vnew_ref, gv_ref,
                         h_scr, u_scr, vn_scr, q_scr, k_scr, v_scr, a_scr, b_scr, m_scr, g_scr):
    n_seq = TILE // CHUNK
    k, v, vn = _project(x_ref, lng_ref, win_ref, lvg_ref, lvb_ref, h_scr, u_scr, vn_scr, q_scr)
    gv_ref[...] = vn
    for c in range(n_seq):
        knew_ref[c] = k[c * CHUNK:(c + 1) * CHUNK]
        vnew_ref[c] = v[c * CHUNK:(c + 1) * CHUNK]

    for g in range(GMLP_GROUPS):
        _gmlp_group(g, CHUNK, ws_ref, bsb_ref, None, u_scr, vn_scr, a_scr)

    kb, vb = _kv_blocks(k, v)
    for c in range(n_seq):
        ckb, cvb = _kv_blocks(ck_ref[c], cv_ref[c])
        k_scr[c, 0:WINDOW, :] = ckb
        v_scr[c, 0:WINDOW, :] = cvb
        k_scr[c, WINDOW:, :] = kb[c * CHUNK:(c + 1) * CHUNK]
        v_scr[c, WINDOW:, :] = vb[c * CHUNK:(c + 1) * CHUNK]
    for c in range(n_seq):
        r0 = c * CHUNK
        scores = _scores(q_scr[r0:r0 + CHUNK, :], lambda j: k_scr[c, :, j * LANES:(j + 1) * LANES])
        _gate_unit(c, h_scr, win_ref, g_scr)
        outs = _softmax_pv(scores, lambda j: v_scr[c, :, j * LANES:(j + 1) * LANES], sink_ref, bias_ref, None)
        b_scr[r0:r0 + CHUNK, :] = jnp.concatenate(outs, axis=1).astype(BF16)

    assert n_seq == 2 * D_MODEL // UNIT
    for i in range(D_MODEL // UNIT):
        _merge_unit(i, g_scr, a_scr, b_scr, m_scr, wpa_ref, wpb_ref)
    for i in range(D_MODEL // UNIT):
        _out_unit(i, x_ref, m_scr, wo_ref, x1_ref)


def _up_dots(h2_scr, wup_ref, c0):
    a = jnp.dot(h2_scr[...], wup_ref[:, c0:c0 + FF_CHUNK], preferred_element_type=F32)
    val = jnp.dot(h2_scr[...], wup_ref[:, D_FF + c0:D_FF + c0 + FF_CHUNK], preferred_element_type=F32)
    return a, val


def _conv_taps(a_s, buf, w):
    n = a_s.shape[0]
    return (w[0:1] * buf[SUBLANES - 2:SUBLANES - 2 + n, :] + w[1:2] * buf[SUBLANES - 1:SUBLANES - 1 + n, :]) \
        + w[2:3] * a_s


def _ffn_kernel(*refs, sample):
    if sample:
        (x1_ref, lnf_ref, wup_ref, wconv_ref, bconv_ref, wdown_ref, lnfin_ref, cconv_ref,
         y_ref, cnew_ref, h2_scr, g_scr, a_buf) = refs
        seq_len = CHUNK
    else:
        (x1_ref, lnf_ref, wup_ref, wconv_ref, bconv_ref, wdown_ref, lnfin_ref,
         y_ref, cnew_ref, h2_scr, g_scr, a_buf, hist_scr) = refs
        seq_len = FFN_TILE

        @pl.when(pl.program_id(1) == 0)
        def _():
            hist_scr[...] = jnp.zeros(hist_scr.shape, F32)

    n_seq = FFN_TILE // seq_len
    row_blocks = [slice(r, r + FFN_EDGE_ROWS) for r in range(0, FFN_TILE, FFN_EDGE_ROWS)]
    for rows in row_blocks:
        h2_scr[rows, :] = _rms_norm(x1_ref[rows, :], lnf_ref[...]).astype(BF16)
    for j, c0 in enumerate(range(0, D_FF, FF_CHUNK)):
        cols = slice(c0, c0 + FF_CHUNK)
        if j == 0:
            parts = [_up_dots(h2_scr.at[rows, :], wup_ref, c0) for rows in row_blocks]
            a = jnp.concatenate([p[0] for p in parts], axis=0)
            val = jnp.concatenate([p[1] for p in parts], axis=0)
        else:
            a, val = _up_dots(h2_scr, wup_ref, c0)
        w = wconv_ref[:, cols]
        taps = []
        for s in range(n_seq):
            a_s = a[s * seq_len:(s + 1) * seq_len]
            buf = a_buf.at[s]
            if sample:
                buf[SUBLANES - (CONV_W - 1):SUBLANES, :] = cconv_ref[s, :, cols]
            else:
                buf[0:SUBLANES, :] = hist_scr[:, cols]
                hist_scr[:, cols] = a_s[seq_len - SUBLANES:]
            buf[SUBLANES:, :] = a_s
            cnew_ref[s, :, cols] = a_s[seq_len - (CONV_W - 1):]
            taps.append(_conv_taps(a_s, buf, w))
        conv = bconv_ref[:, cols] + (taps[0] if n_seq == 1 else jnp.concatenate(taps, axis=0))
        g_scr[:, cols] = (jax.nn.gelu(conv) * val).astype(BF16)
    for rows in row_blocks:
        x2 = x1_ref[rows, :] + jnp.dot(g_scr[rows, :], wdown_ref[...], preferred_element_type=F32)
        y_ref[rows, :] = _rms_norm(x2, lnfin_ref[...])


def _resident(shape):
    nd = len(shape)
    return pl.BlockSpec(shape, lambda *_: (0,) * nd, pipeline_mode=pl.Buffered(1))


def _alibi_bias():
    slopes = jnp.exp2(-8.0 * jnp.arange(1, N_HEADS + 1, dtype=F32) / N_HEADS)
    qi = jnp.arange(CHUNK)[:, None]
    kj = jnp.arange(KEYS)[None, :]
    dist = jnp.abs(qi + WINDOW - kj).astype(F32)
    return slopes[:, None, None] * dist[None]


def _mixer(x2d, n_batch, params, caches):
    lng, win, lvg, lvb, ws, bsb, sinks, bias, wpa, wpb, wo = params
    tokens = x2d.shape[0]
    sample = caches is not None
    common_in = [
        None,
        _resident(lng.shape), _resident(win.shape), _resident(lvg.shape), _resident(lvb.shape),
        _resident(ws.shape), _resident(bsb.shape),
        pl.BlockSpec(memory_space=pltpu.SMEM),
        _resident(bias.shape), _resident(wpa.shape), _resident(wpb.shape), _resident(wo.shape),
    ]
    common_scratch = [
        pltpu.VMEM((TILE, D_MODEL), BF16),
        pltpu.VMEM((TILE, GMLP_WIDTH), F32),
        pltpu.VMEM((TILE, GMLP_WIDTH), BF16),
        pltpu.VMEM((TILE, N_HEADS * LANES), BF16),
    ]
    tail_scratch = [
        pltpu.VMEM((TILE, GMLP_WIDTH), BF16),
        pltpu.VMEM((TILE, ATTN_WIDTH), BF16),
        pltpu.VMEM((TILE, D_MODEL), BF16),
        pltpu.VMEM((TILE, 2 * D_MODEL), F32),
    ]
    if not sample:
        tiles_per_seq = tokens // n_batch // TILE
        n_tiles = n_batch * tiles_per_seq
        grid = (n_tiles + 1,)
        cur = lambda t: jnp.minimum(t, n_tiles - 1)
        prev = lambda t: jnp.maximum(t - 1, 0)
        tok_cur = pl.BlockSpec((TILE, D_MODEL), lambda t: (cur(t), 0))
        tok_prev = pl.BlockSpec((TILE, D_MODEL), lambda t: (prev(t), 0))
        kv_out = pl.BlockSpec((1, WINDOW, KV_WIDTH), lambda t: (cur(t) // tiles_per_seq, 0, 0))
        in_specs = [tok_cur, tok_prev] + common_in[1:]
        out_specs = [tok_prev, kv_out, kv_out]
        out_shape = [jax.ShapeDtypeStruct((tokens, D_MODEL), F32),
                     jax.ShapeDtypeStruct((n_batch, WINDOW, KV_WIDTH), F32),
                     jax.ShapeDtypeStruct((n_batch, WINDOW, KV_WIDTH), F32)]
        kv_scratch = [pltpu.VMEM((WINDOW + TILE, N_KV_HEADS * LANES), BF16)] * 2
        merge_slots = sum(1 for slot in PROMPT_SLOTS if any(kind == "m" for kind, _ in slot))
        tail_scratch = tail_scratch + [pltpu.VMEM((merge_slots * CHUNK, ATTN_WIDTH), BF16)]
        body = functools.partial(_mixer_prompt_kernel, n_steps=n_tiles + 1, tiles_per_seq=tiles_per_seq)
        args, lead, name = (), (x2d, x2d), "mixer_prompt"
    else:
        n_seq = TILE // CHUNK
        grid = (tokens // TILE,)
        tok = pl.BlockSpec((TILE, D_MODEL), lambda t: (t, 0))
        cache_spec = pl.BlockSpec((n_seq, WINDOW, KV_WIDTH), lambda t: (t, 0, 0))
        kv_out = pl.BlockSpec((n_seq, CHUNK, KV_WIDTH), lambda t: (t, 0, 0))
        in_specs = [tok] + common_in[1:] + [cache_spec, cache_spec]
        out_specs = [tok, kv_out, kv_out, pl.BlockSpec((TILE, GMLP_WIDTH), lambda t: (t, 0))]
        out_shape = [jax.ShapeDtypeStruct((tokens, D_MODEL), F32),
                     jax.ShapeDtypeStruct((n_batch, CHUNK, KV_WIDTH), F32),
                     jax.ShapeDtypeStruct((n_batch, CHUNK, KV_WIDTH), F32),
                     jax.ShapeDtypeStruct((tokens, GMLP_WIDTH), F32)]
        kv_scratch = [pltpu.VMEM((n_seq, KEYS, N_KV_HEADS * LANES), BF16)] * 2
        body, args, lead, name = _mixer_sample_kernel, tuple(caches), (x2d,), "mixer_sample"
    return pl.pallas_call(
        body,
        grid=grid,
        in_specs=in_specs,
        out_specs=out_specs,
        out_shape=out_shape,
        scratch_shapes=common_scratch + kv_scratch + tail_scratch,
        compiler_params=pltpu.CompilerParams(
            dimension_semantics=("arbitrary",) * len(grid), vmem_limit_bytes=VMEM_LIMIT_BYTES),
        name=name,
    )(*lead, lng, win, lvg, lvb, ws, bsb, sinks, bias, wpa, wpb, wo, *args)


def _ffn(x2d, n_batch, params, cache_conv):
    lnf, wup, wconv, bconv, wdown, lnfin = params
    tokens = x2d.shape[0]
    sample = cache_conv is not None
    weights_in = [_resident(lnf.shape), _resident(wup.shape), _resident(wconv.shape), _resident(bconv.shape),
                  _resident(wdown.shape), _resident(lnfin.shape)]
    scratch = [pltpu.VMEM((FFN_TILE, D_MODEL), BF16),
               pltpu.VMEM((FFN_TILE, D_FF), BF16)]
    if not sample:
        n_tiles = tokens // n_batch // FFN_TILE
        grid = (n_batch, n_tiles)
        tok = pl.BlockSpec((FFN_TILE, D_MODEL), lambda b, s: (b * n_tiles + s, 0))
        in_specs = [tok] + weights_in
        conv_out = pl.BlockSpec((1, CONV_W - 1, D_FF), lambda b, s: (b, 0, 0))
        scratch += [pltpu.VMEM((1, SUBLANES + FFN_TILE, FF_CHUNK), F32),
                    pltpu.VMEM((SUBLANES, D_FF), F32)]
        args, name = (), "ffn_prompt"
    else:
        n_seq = FFN_TILE // CHUNK
        grid = (tokens // FFN_TILE,)
        tok = pl.BlockSpec((FFN_TILE, D_MODEL), lambda t: (t, 0))
        conv_out = pl.BlockSpec((n_seq, CONV_W - 1, D_FF), lambda t: (t, 0, 0))
        in_specs = [tok] + weights_in + [conv_out]
        scratch += [pltpu.VMEM((n_seq, SUBLANES + CHUNK, FF_CHUNK), F32)]
        args, name = (cache_conv,), "ffn_sample"
    return pl.pallas_call(
        functools.partial(_ffn_kernel, sample=sample),
        grid=grid,
        in_specs=in_specs,
        out_specs=[tok, conv_out],
        out_shape=[jax.ShapeDtypeStruct((tokens, D_MODEL), F32),
                   jax.ShapeDtypeStruct((n_batch, CONV_W - 1, D_FF), F32)],
        scratch_shapes=scratch,
        compiler_params=pltpu.CompilerParams(
            dimension_semantics=("arbitrary",) * len(grid), vmem_limit_bytes=VMEM_LIMIT_BYTES),
        name=name,
    )(x2d, lnf, wup, wconv, bconv, wdown, lnfin, *args)


def kernel(x_prompt, x_sample, cache_k, cache_v, cache_conv, ln_mix_g, w_in, ln_v_g, ln_v_b, w_s, b_s,
           attn_sinks, w_pa, w_pb, w_o, ln_ffn_g, w_up, w_conv, b_conv, w_down, ln_final_g):
    depth = w_in.shape[0]
    assert depth == 1, "single-layer kernel"
    n_prompt, seq, _ = x_prompt.shape
    n_sample, dec_seq, _ = x_sample.shape
    assert seq % FFN_TILE == 0 and seq % TILE == 0 and dec_seq == CHUNK
    assert (n_sample * dec_seq) % FFN_TILE == 0 and (n_sample * dec_seq) % TILE == 0
    assert cache_k.shape[2] == WINDOW

    row = lambda p: p.reshape(1, -1)
    mixer_params = (
        row(ln_mix_g[0]), w_in[0].astype(BF16), row(ln_v_g[0]), row(ln_v_b[0]),
        w_s[0], jnp.broadcast_to(b_s[0][:, :, None], (GMLP_GROUPS, GMLP_CHUNK, LANES)),
        attn_sinks[0], _alibi_bias(),
        w_pa[0].astype(BF16), w_pb[0].astype(BF16), w_o[0].astype(BF16))
    ffn_params = (row(ln_ffn_g[0]), w_up[0].astype(BF16), w_conv[0], row(b_conv[0]),
                  w_down[0].astype(BF16), row(ln_final_g))

    xp = x_prompt.reshape(n_prompt * seq, D_MODEL)
    x1p, kp, vp = _mixer(xp, n_prompt, mixer_params, None)
    yp, convp = _ffn(x1p, n_prompt, ffn_params, None)

    xs = x_sample.reshape(n_sample * dec_seq, D_MODEL)
    caches = (cache_k[0].reshape(n_sample, WINDOW, KV_WIDTH), cache_v[0].reshape(n_sample, WINDOW, KV_WIDTH))
    x1s, ks, vs, gvs = _mixer(xs, n_sample, mixer_params, caches)
    ys, convs = _ffn(x1s, n_sample, ffn_params, cache_conv[0])

    kv_shape = lambda a: a.reshape(1, a.shape[0], a.shape[1], N_KV_HEADS, HEAD_DIM)
    return (yp.reshape(n_prompt, seq, D_MODEL), ys.reshape(n_sample, dec_seq, D_MODEL),
            kv_shape(kp), kv_shape(vp), convp[None],
            kv_shape(ks), kv_shape(vs), convs[None],
            gvs.reshape(1, n_sample, dec_seq, GMLP_WIDTH))
```

```python
import functools

import jax
import jax.numpy as jnp
from jax import lax
from jax.experimental import pallas as pl
from jax.experimental.pallas import tpu as pltpu

F32 = jnp.float32
BF16 = jnp.bfloat16

D_MODEL = 1024
GMLP_WIDTH = 1024
GMLP_GROUPS = 8
GMLP_CHUNK = 128
N_HEADS = 16
N_KV_HEADS = 4
GQA_GROUP = N_HEADS // N_KV_HEADS
HEAD_DIM = 64
ATTN_WIDTH = N_HEADS * HEAD_DIM
KV_WIDTH = N_KV_HEADS * HEAD_DIM
CHUNK = 64
WINDOW = 128
KEYS = WINDOW + CHUNK
D_FF = 2816
CONV_W = 3
EPS = 1e-6
NEG_INF = -1e30
COL_U = 0
COL_VG = COL_U + GMLP_WIDTH
COL_Q = COL_VG + GMLP_WIDTH
COL_K = COL_Q + ATTN_WIDTH
COL_V = COL_K + KV_WIDTH
COL_GATE = COL_V + KV_WIDTH
IN_COLS = COL_GATE + 2 * D_MODEL

LANES = 128
SUBLANES = 8
HALF = LANES // 2

TILE = 512
FF_CHUNK = 256
UNIT = 256
PROMPT_SLOTS = (
    (("m", 0),), (("m", 1),), (("m", 2),), (("m", 3),),
    (("o", 0), ("o", 1), ("g", 0)), (("o", 2), ("o", 3), ("g", 1)),
    (("g", 2), ("g", 3), ("g", 4)), (("g", 5), ("g", 6), ("g", 7)),
)
FFN_TILE = 1024
FFN_EDGE_ROWS = 256
VMEM_LIMIT_BYTES = 52 * 1024 * 1024


def _rms_norm(x, g):
    return x * lax.rsqrt(jnp.mean(x * x, axis=-1, keepdims=True) + EPS) * g


def _layer_norm(x, g, b):
    mu = jnp.mean(x, axis=-1, keepdims=True)
    xc = x - mu
    return xc * lax.rsqrt(jnp.mean(xc * xc, axis=-1, keepdims=True) + EPS) * g + b


def _col(x, c):
    return x[:, c * LANES:(c + 1) * LANES]


def _swap_halves(x):
    n = x.shape[1] // LANES
    return jnp.concatenate([pltpu.roll(_col(x, c), HALF, axis=1) for c in range(n)], axis=1)


def _low_half_mask():
    return lax.broadcasted_iota(jnp.int32, (1, LANES), 1) < HALF


def _project(x_ref, lng_ref, win_ref, lvg_ref, lvb_ref, h_scr, u_scr, vn_scr, q_scr):
    h_scr[...] = _rms_norm(x_ref[...], lng_ref[...]).astype(BF16)

    def proj(c0, n):
        return jnp.dot(h_scr[...], win_ref[:, c0:c0 + n], preferred_element_type=F32)

    u_scr[...] = jax.nn.gelu(proj(COL_U, GMLP_WIDTH))
    vn = _layer_norm(jax.nn.gelu(proj(COL_VG, GMLP_WIDTH)), lvg_ref[...], lvb_ref[...])
    vn_scr[...] = vn.astype(BF16)
    q = proj(COL_Q, ATTN_WIDTH) * (HEAD_DIM ** -0.5)
    qs = _swap_halves(q)
    blocks = []
    for c in range(ATTN_WIDTH // LANES):
        blocks += [_col(q, c), _col(qs, c)]
    q_scr[...] = jnp.concatenate(blocks, axis=1).astype(BF16)
    kv = proj(COL_K, 2 * KV_WIDTH)
    return kv[:, :KV_WIDTH], kv[:, KV_WIDTH:], vn


def _kv_blocks(k, v):
    lo = _low_half_mask()
    ks, vs = _swap_halves(k), _swap_halves(v)
    kb, vb = [], []
    for j in range(N_KV_HEADS):
        c = j // 2
        if j % 2 == 0:
            kb.append(jnp.where(lo, _col(k, c), 0.0))
            vb.append(jnp.where(lo, _col(v, c), _col(vs, c)))
        else:
            kb.append(jnp.where(lo, _col(ks, c), 0.0))
            vb.append(jnp.where(lo, _col(vs, c), _col(v, c)))
    return jnp.concatenate(kb, axis=1).astype(BF16), jnp.concatenate(vb, axis=1).astype(BF16)


def _scores(qc, kctx, kv_heads=range(N_KV_HEADS)):
    out = []
    for j in kv_heads:
        qs = jnp.concatenate([_col(qc, GQA_GROUP * j + g) for g in range(GQA_GROUP)], axis=0)
        out.append(lax.dot_general(qs, kctx(j), (((1,), (1,)), ((), ())), preferred_element_type=F32))
    return out


def _softmax_pv(scores, vctx, sink_ref, bias_ref, first_valid_key, kv_heads=range(N_KV_HEADS)):
    lo = _low_half_mask()
    ones = jnp.ones((KEYS, LANES), BF16)
    outs = []
    for s, j in zip(scores, kv_heads):
        heads = [GQA_GROUP * j + g for g in range(GQA_GROUP)]
        es, ms = [], []
        for g, h in enumerate(heads):
            sg = s[g * CHUNK:(g + 1) * CHUNK] - bias_ref[h]
            if first_valid_key is not None:
                kj = lax.broadcasted_iota(jnp.int32, (1, KEYS), 1)
                sg = jnp.where(kj >= first_valid_key, sg, NEG_INF)
            m = jnp.maximum(jnp.max(sg, axis=-1, keepdims=True), sink_ref[h])
            es.append(jnp.exp(sg - m).astype(BF16))
            ms.append(m)
        o = jnp.dot(jnp.concatenate(es, axis=0), jnp.concatenate([vctx(j), ones], axis=1),
                    preferred_element_type=F32)
        for p in range(GQA_GROUP // 2):
            g0, g1 = 2 * p, 2 * p + 1
            o0, o1 = o[g0 * CHUNK:(g0 + 1) * CHUNK], o[g1 * CHUNK:(g1 + 1) * CHUNK]
            num = jnp.where(lo, o0[:, :LANES], o1[:, :LANES])
            den = jnp.where(lo, o0[:, LANES:], o1[:, LANES:])
            sink_term = jnp.where(lo, jnp.exp(sink_ref[heads[g0]] - ms[g0]),
                                  jnp.exp(sink_ref[heads[g1]] - ms[g1]))
            outs.append(num / (den + sink_term))
    return outs


def _gmlp_group(g, n_rows, ws_ref, bsb_ref, mask, u_scr, vn_scr, a_out):
    n_blk = TILE // n_rows
    w = ws_ref[g][:n_rows, :n_rows]
    wm = (w if mask is None else w * mask).astype(BF16)
    cols = slice(g * LANES, (g + 1) * LANES)
    vcat = jnp.concatenate([vn_scr[n * n_rows:(n + 1) * n_rows, cols] for n in range(n_blk)], axis=1)
    sp = jnp.dot(wm, vcat, preferred_element_type=F32)
    bs = bsb_ref[g][:n_rows]
    for n in range(n_blk):
        rows = slice(n * n_rows, (n + 1) * n_rows)
        a_out[rows, cols] = (u_scr[rows, cols] * (_col(sp, n) + bs)).astype(BF16)


def _gate_unit(k, h_scr, win_ref, g_scr):
    cols = slice(k * UNIT, (k + 1) * UNIT)
    g_scr[:, cols] = jax.nn.sigmoid(
        jnp.dot(h_scr[...], win_ref[:, COL_GATE + k * UNIT:COL_GATE + (k + 1) * UNIT], preferred_element_type=F32))


def _merge_unit(i, g_scr, a_scr, b_scr, m_scr, wpa_ref, wpb_ref):
    cols = slice(i * UNIT, (i + 1) * UNIT)
    ya = jnp.dot(a_scr[...], wpa_ref[:, cols], preferred_element_type=F32)
    yb = jnp.dot(b_scr[...], wpb_ref[:, cols], preferred_element_type=F32)
    m_scr[:, cols] = (g_scr[:, cols] * ya + g_scr[:, D_MODEL + i * UNIT:D_MODEL + (i + 1) * UNIT] * yb).astype(BF16)


def _out_unit(i, x_ref, m_scr, wo_ref, x1_ref):
    cols = slice(i * UNIT, (i + 1) * UNIT)
    x1_ref[:, cols] = x_ref[:, cols] + jnp.dot(m_scr[...], wo_ref[:, cols], preferred_element_type=F32)


def _mixer_prompt_kernel(x_ref, lng_ref, win_ref, lvg_ref, lvb_ref, ws_ref, bsb_ref, sink_ref, bias_ref,
                         wpa_ref, wpb_ref, wo_ref,
                         x1_ref, knew_ref, vnew_ref,
                         h_scr, u_scr, vn_scr, q_scr, k_scr, v_scr, a_scr, b_scr, m_scr, g_scr, a_new, b_new,
                         xprev_ref, *, n_steps, tiles_per_seq):
    t = pl.program_id(0)
    seq_tile = lax.rem(jnp.minimum(t, n_steps - 2), tiles_per_seq)

    @pl.when(t == 0)
    def _():
        a_scr[...] = jnp.zeros(a_scr.shape, BF16)
        b_scr[...] = jnp.zeros(b_scr.shape, BF16)
        g_scr[...] = jnp.zeros(g_scr.shape, F32)
        xprev_ref[...] = jnp.zeros(xprev_ref.shape, F32)
        k_scr[0:WINDOW, :] = jnp.zeros((WINDOW, N_KV_HEADS * LANES), BF16)
        v_scr[0:WINDOW, :] = jnp.zeros((WINDOW, N_KV_HEADS * LANES), BF16)

    def run_unit(kind, i):
        if kind == "m":
            _merge_unit(i, g_scr, a_scr, b_scr, m_scr, wpa_ref, wpb_ref)
        elif kind == "o":
            _out_unit(i, xprev_ref, m_scr, wo_ref, x1_ref)
        else:
            _gate_unit(i, h_scr, win_ref, g_scr)

    k, v, _ = _project(x_ref, lng_ref, win_ref, lvg_ref, lvb_ref, h_scr, u_scr, vn_scr, q_scr)
    knew_ref[0] = k[TILE - WINDOW:]
    vnew_ref[0] = v[TILE - WINDOW:]

    ri = lax.broadcasted_iota(jnp.int32, (GMLP_CHUNK, GMLP_CHUNK), 0)
    ci = lax.broadcasted_iota(jnp.int32, (GMLP_CHUNK, GMLP_CHUNK), 1)
    causal = jnp.where((ci < CHUNK) | (ri >= CHUNK), 1.0, 0.0).astype(F32)
    for g in range(GMLP_GROUPS):
        _gmlp_group(g, GMLP_CHUNK, ws_ref, bsb_ref, causal, u_scr, vn_scr, a_new)

    kb, vb = _kv_blocks(k, v)
    k_scr[WINDOW:, :] = kb
    v_scr[WINDOW:, :] = vb

    for c in range(TILE // CHUNK):
        r0 = c * CHUNK
        first_valid = None
        if c * CHUNK < WINDOW:
            first_valid = jnp.where(seq_tile == 0, WINDOW - c * CHUNK, 0)
        scores = _scores(q_scr[r0:r0 + CHUNK, :], lambda j: k_scr[r0:r0 + KEYS, j * LANES:(j + 1) * LANES])
        for kind, i in PROMPT_SLOTS[c]:
            run_unit(kind, i)
        outs = _softmax_pv(scores, lambda j: v_scr[r0:r0 + KEYS, j * LANES:(j + 1) * LANES],
                           sink_ref, bias_ref, first_valid)
        b_new[r0:r0 + CHUNK, :] = jnp.concatenate(outs, axis=1).astype(BF16)
    k_scr[0:WINDOW, :] = k_scr[TILE:TILE + WINDOW, :]
    v_scr[0:WINDOW, :] = v_scr[TILE:TILE + WINDOW, :]
    a_scr[...] = a_new[...]
    b_scr[...] = b_new[...]
    xprev_ref[...] = x_ref[...]


def _mixer_sample_kernel(x_ref, lng_ref, win_ref, lvg_ref, lvb_ref, ws_ref, bsb_ref, sink_ref, bias_ref,
                         wpa_ref, wpb_ref, wo_ref, ck_ref, cv_ref,
                         x1_ref, knew_ref, vnew_ref, gv_ref,
                         h_scr, u_scr, vn_scr, q_scr, k_scr, v_scr, a_scr, b_scr, m_scr, g_scr):
    n_seq = TILE // CHUNK
    k, v, vn = _project(x_ref, lng_ref, win_ref, lvg_ref, lvb_ref, h_scr, u_scr, vn_scr, q_scr)
    gv_ref[...] = vn
    for c in range(n_seq):
        knew_ref[c] = k[c * CHUNK:(c + 1) * CHUNK]
        vnew_ref[c] = v[c * CHUNK:(c + 1) * CHUNK]

    for g in range(GMLP_GROUPS):
        _gmlp_group(g, CHUNK, ws_ref, bsb_ref, None, u_scr, vn_scr, a_scr)

    kb, vb = _kv_blocks(k, v)
    for c in range(n_seq):
        ckb, cvb = _kv_blocks(ck_ref[c], cv_ref[c])
        k_scr[c, 0:WINDOW, :] = ckb
        v_scr[c, 0:WINDOW, :] = cvb
        k_scr[c, WINDOW:, :] = kb[c * CHUNK:(c + 1) * CHUNK]
        v_scr[c, WINDOW:, :] = vb[c * CHUNK:(c + 1) * CHUNK]
    for c in range(n_seq):
        r0 = c * CHUNK
        scores = _scores(q_scr[r0:r0 + CHUNK, :], lambda j: k_scr[c, :, j * LANES:(j + 1) * LANES])
        _gate_unit(c, h_scr, win_ref, g_scr)
        outs = _softmax_pv(scores, lambda j: v_scr[c, :, j * LANES:(j + 1) * LANES], sink_ref, bias_ref, None)
        b_scr[r0:r0 + CHUNK, :] = jnp.concatenate(outs, axis=1).astype(BF16)

    assert n_seq == 2 * D_MODEL // UNIT
    for i in range(D_MODEL // UNIT):
        _merge_unit(i, g_scr, a_scr, b_scr, m_scr, wpa_ref, wpb_ref)
    for i in range(D_MODEL // UNIT):
        _out_unit(i, x_ref, m_scr, wo_ref, x1_ref)


def _up_dots(h2_scr, wup_ref, c0):
    a = jnp.dot(h2_scr[...], wup_ref[:, c0:c0 + FF_CHUNK], preferred_element_type=F32)
    val = jnp.dot(h2_scr[...], wup_ref[:, D_FF + c0:D_FF + c0 + FF_CHUNK], preferred_element_type=F32)
    return a, val


def _conv_taps(a_s, buf, w):
    n = a_s.shape[0]
    return (w[0:1] * buf[SUBLANES - 2:SUBLANES - 2 + n, :] + w[1:2] * buf[SUBLANES - 1:SUBLANES - 1 + n, :]) \
        + w[2:3] * a_s


def _ffn_kernel(*refs, sample):
    if sample:
        (x1_ref, lnf_ref, wup_ref, wconv_ref, bconv_ref, wdown_ref, lnfin_ref, cconv_ref,
         y_ref, cnew_ref, h2_scr, g_scr, a_buf) = refs
        seq_len = CHUNK
    else:
        (x1_ref, lnf_ref, wup_ref, wconv_ref, bconv_ref, wdown_ref, lnfin_ref,
         y_ref, cnew_ref, h2_scr, g_scr, a_buf, hist_scr) = refs
        seq_len = FFN_TILE

        @pl.when(pl.program_id(1) == 0)
        def _():
            hist_scr[...] = jnp.zeros(hist_scr.shape, F32)

    n_seq = FFN_TILE // seq_len
    row_blocks = [slice(r, r + FFN_EDGE_ROWS) for r in range(0, FFN_TILE, FFN_EDGE_ROWS)]
    for rows in row_blocks:
        h2_scr[rows, :] = _rms_norm(x1_ref[rows, :], lnf_ref[...]).astype(BF16)
    for j, c0 in enumerate(range(0, D_FF, FF_CHUNK)):
        cols = slice(c0, c0 + FF_CHUNK)
        if j == 0:
            parts = [_up_dots(h2_scr.at[rows, :], wup_ref, c0) for rows in row_blocks]
            a = jnp.concatenate([p[0] for p in parts], axis=0)
            val = jnp.concatenate([p[1] for p in parts], axis=0)
        else:
            a, val = _up_dots(h2_scr, wup_ref, c0)
        w = wconv_ref[:, cols]
        taps = []
        for s in range(n_seq):
            a_s = a[s * seq_len:(s + 1) * seq_len]
            buf = a_buf.at[s]
            if sample:
                buf[SUBLANES - (CONV_W - 1):SUBLANES, :] = cconv_ref[s, :, cols]
            else:
                buf[0:SUBLANES, :] = hist_scr[:, cols]
                hist_scr[:, cols] = a_s[seq_len - SUBLANES:]
            buf[SUBLANES:, :] = a_s
            cnew_ref[s, :, cols] = a_s[seq_len - (CONV_W - 1):]
            taps.append(_conv_taps(a_s, buf, w))
        conv = bconv_ref[:, cols] + (taps[0] if n_seq == 1 else jnp.concatenate(taps, axis=0))
        g_scr[:, cols] = (jax.nn.gelu(conv) * val).astype(BF16)
    for rows in row_blocks:
        x2 = x1_ref[rows, :] + jnp.dot(g_scr[rows, :], wdown_ref[...], preferred_element_type=F32)
        y_ref[rows, :] = _rms_norm(x2, lnfin_ref[...])


def _resident(shape):
    nd = len(shape)
    return pl.BlockSpec(shape, lambda *_: (0,) * nd, pipeline_mode=pl.Buffered(1))


def _alibi_bias():
    slopes = jnp.exp2(-8.0 * jnp.arange(1, N_HEADS + 1, dtype=F32) / N_HEADS)
    qi = jnp.arange(CHUNK)[:, None]
    kj = jnp.arange(KEYS)[None, :]
    dist = jnp.abs(qi + WINDOW - kj).astype(F32)
    return slopes[:, None, None] * dist[None]


def _mixer(x2d, n_batch, params, caches):
    lng, win, lvg, lvb, ws, bsb, sinks, bias, wpa, wpb, wo = params
    tokens = x2d.shape[0]
    sample = caches is not None
    common_in = [
        None,
        _resident(lng.shape), _resident(win.shape), _resident(lvg.shape), _resident(lvb.shape),
        _resident(ws.shape), _resident(bsb.shape),
        pl.BlockSpec(memory_space=pltpu.SMEM),
        _resident(bias.shape), _resident(wpa.shape), _resident(wpb.shape), _resident(wo.shape),
    ]
    common_scratch = [
        pltpu.VMEM((TILE, D_MODEL), BF16),
        pltpu.VMEM((TILE, GMLP_WIDTH), F32),
        pltpu.VMEM((TILE, GMLP_WIDTH), BF16),
        pltpu.VMEM((TILE, N_HEADS * LANES), BF16),
    ]
    tail_scratch = [
        pltpu.VMEM((TILE, GMLP_WIDTH), BF16),
        pltpu.VMEM((TILE, ATTN_WIDTH), BF16),
        pltpu.VMEM((TILE, D_MODEL), BF16),
        pltpu.VMEM((TILE, 2 * D_MODEL), F32),
    ]
    if not sample:
        tiles_per_seq = tokens // n_batch // TILE
        n_tiles = n_batch * tiles_per_seq
        grid = (n_tiles + 1,)
        cur = lambda t: jnp.minimum(t, n_tiles - 1)
        prev = lambda t: jnp.maximum(t - 1, 0)
        tok_cur = pl.BlockSpec((TILE, D_MODEL), lambda t: (cur(t), 0))
        tok_prev = pl.BlockSpec((TILE, D_MODEL), lambda t: (prev(t), 0))
        kv_out = pl.BlockSpec((1, WINDOW, KV_WIDTH), lambda t: (cur(t) // tiles_per_seq, 0, 0))
        in_specs = [tok_cur] + common_in[1:]
        out_specs = [tok_prev, kv_out, kv_out]
        out_shape = [jax.ShapeDtypeStruct((tokens, D_MODEL), F32),
                     jax.ShapeDtypeStruct((n_batch, WINDOW, KV_WIDTH), F32),
                     jax.ShapeDtypeStruct((n_batch, WINDOW, KV_WIDTH), F32)]
        kv_scratch = [pltpu.VMEM((WINDOW + TILE, N_KV_HEADS * LANES), BF16)] * 2
        tail_scratch = tail_scratch + [pltpu.VMEM((TILE, GMLP_WIDTH), BF16),
                                       pltpu.VMEM((TILE, ATTN_WIDTH), BF16),
                                       pltpu.VMEM((TILE, D_MODEL), F32)]
        body = functools.partial(_mixer_prompt_kernel, n_steps=n_tiles + 1, tiles_per_seq=tiles_per_seq)
        args, lead, name = (), (x2d,), "mixer_prompt"
    else:
        n_seq = TILE // CHUNK
        grid = (tokens // TILE,)
        tok = pl.BlockSpec((TILE, D_MODEL), lambda t: (t, 0))
        cache_spec = pl.BlockSpec((n_seq, WINDOW, KV_WIDTH), lambda t: (t, 0, 0))
        kv_out = pl.BlockSpec((n_seq, CHUNK, KV_WIDTH), lambda t: (t, 0, 0))
        in_specs = [tok] + common_in[1:] + [cache_spec, cache_spec]
        out_specs = [tok, kv_out, kv_out, pl.BlockSpec((TILE, GMLP_WIDTH), lambda t: (t, 0))]
        out_shape = [jax.ShapeDtypeStruct((tokens, D_MODEL), F32),
                     jax.ShapeDtypeStruct((n_batch, CHUNK, KV_WIDTH), F32),
                     jax.ShapeDtypeStruct((n_batch, CHUNK, KV_WIDTH), F32),
                     jax.ShapeDtypeStruct((tokens, GMLP_WIDTH), F32)]
        kv_scratch = [pltpu.VMEM((n_seq, KEYS, N_KV_HEADS * LANES), BF16)] * 2
        body, args, lead, name = _mixer_sample_kernel, tuple(caches), (x2d,), "mixer_sample"
    return pl.pallas_call(
        body,
        grid=grid,
        in_specs=in_specs,
        out_specs=out_specs,
        out_shape=out_shape,
        scratch_shapes=common_scratch + kv_scratch + tail_scratch,
        compiler_params=pltpu.CompilerParams(
            dimension_semantics=("arbitrary",) * len(grid), vmem_limit_bytes=VMEM_LIMIT_BYTES),
        name=name,
    )(*lead, lng, win, lvg, lvb, ws, bsb, sinks, bias, wpa, wpb, wo, *args)


def _ffn(x2d, n_batch, params, cache_conv):
    lnf, wup, wconv, bconv, wdown, lnfin = params
    tokens = x2d.shape[0]
    sample = cache_conv is not None
    weights_in = [_resident(lnf.shape), _resident(wup.shape), _resident(wconv.shape), _resident(bconv.shape),
                  _resident(wdown.shape), _resident(lnfin.shape)]
    scratch = [pltpu.VMEM((FFN_TILE, D_MODEL), BF16),
               pltpu.VMEM((FFN_TILE, D_FF), BF16)]
    if not sample:
        n_tiles = tokens // n_batch // FFN_TILE
        grid = (n_batch, n_tiles)
        tok = pl.BlockSpec((FFN_TILE, D_MODEL), lambda b, s: (b * n_tiles + s, 0))
        in_specs = [tok] + weights_in
        conv_out = pl.BlockSpec((1, CONV_W - 1, D_FF), lambda b, s: (b, 0, 0))
        scratch += [pltpu.VMEM((1, SUBLANES + FFN_TILE, FF_CHUNK), F32),
                    pltpu.VMEM((SUBLANES, D_FF), F32)]
        args, name = (), "ffn_prompt"
    else:
        n_seq = FFN_TILE // CHUNK
        grid = (tokens // FFN_TILE,)
        tok = pl.BlockSpec((FFN_TILE, D_MODEL), lambda t: (t, 0))
        conv_out = pl.BlockSpec((n_seq, CONV_W - 1, D_FF), lambda t: (t, 0, 0))
        in_specs = [tok] + weights_in + [conv_out]
        scratch += [pltpu.VMEM((n_seq, SUBLANES + CHUNK, FF_CHUNK), F32)]
        args, name = (cache_conv,), "ffn_sample"
    return pl.pallas_call(
        functools.partial(_ffn_kernel, sample=sample),
        grid=grid,
        in_specs=in_specs,
        out_specs=[tok, conv_out],
        out_shape=[jax.ShapeDtypeStruct((tokens, D_MODEL), F32),
                   jax.ShapeDtypeStruct((n_batch, CONV_W - 1, D_FF), F32)],
        scratch_shapes=scratch,
        compiler_params=pltpu.CompilerParams(
            dimension_semantics=("arbitrary",) * len(grid), vmem_limit_bytes=VMEM_LIMIT_BYTES),
        name=name,
    )(x2d, lnf, wup, wconv, bconv, wdown, lnfin, *args)


def kernel(x_prompt, x_sample, cache_k, cache_v, cache_conv, ln_mix_g, w_in, ln_v_g, ln_v_b, w_s, b_s,
           attn_sinks, w_pa, w_pb, w_o, ln_ffn_g, w_up, w_conv, b_conv, w_down, ln_final_g):
    depth = w_in.shape[0]
    assert depth == 1, "single-layer kernel"
    n_prompt, seq, _ = x_prompt.shape
    n_sample, dec_seq, _ = x_sample.shape
    assert seq % FFN_TILE == 0 and seq % TILE == 0 and dec_seq == CHUNK
    assert (n_sample * dec_seq) % FFN_TILE == 0 and (n_sample * dec_seq) % TILE == 0
    assert cache_k.shape[2] == WINDOW

    row = lambda p: p.reshape(1, -1)
    mixer_params = (
        row(ln_mix_g[0]), w_in[0].astype(BF16), row(ln_v_g[0]), row(ln_v_b[0]),
        w_s[0], jnp.broadcast_to(b_s[0][:, :, None], (GMLP_GROUPS, GMLP_CHUNK, LANES)),
        attn_sinks[0], _alibi_bias(),
        w_pa[0].astype(BF16), w_pb[0].astype(BF16), w_o[0].astype(BF16))
    ffn_params = (row(ln_ffn_g[0]), w_up[0].astype(BF16), w_conv[0], row(b_conv[0]),
                  w_down[0].astype(BF16), row(ln_final_g))

    xp = x_prompt.reshape(n_prompt * seq, D_MODEL)
    x1p, kp, vp = _mixer(xp, n_prompt, mixer_params, None)
    yp, convp = _ffn(x1p, n_prompt, ffn_params, None)

    xs = x_sample.reshape(n_sample * dec_seq, D_MODEL)
    caches = (cache_k[0].reshape(n_sample, WINDOW, KV_WIDTH), cache_v[0].reshape(n_sample, WINDOW, KV_WIDTH))
    x1s, ks, vs, gvs = _mixer(xs, n_sample, mixer_params, caches)
    ys, convs = _ffn(x1s, n_sample, ffn_params, cache_conv[0])

    kv_shape = lambda a: a.reshape(1, a.shape[0], a.shape[1], N_KV_HEADS, HEAD_DIM)
    return (yp.reshape(n_prompt, seq, D_MODEL), ys.reshape(n_sample, dec_seq, D_MODEL),
            kv_shape(kp), kv_shape(vp), convp[None],
            kv_shape(ks), kv_shape(vs), convs[None],
            gvs.reshape(1, n_sample, dec_seq, GMLP_WIDTH))
```

```python
import functools

import jax
import jax.numpy as jnp
from jax import lax
from jax.experimental import pallas as pl
from jax.experimental.pallas import tpu as pltpu

F32 = jnp.float32
BF16 = jnp.bfloat16

D_MODEL = 1024
GMLP_WIDTH = 1024
GMLP_GROUPS = 8
GMLP_CHUNK = 128
N_HEADS = 16
N_KV_HEADS = 4
GQA_GROUP = N_HEADS // N_KV_HEADS
HEAD_DIM = 64
ATTN_WIDTH = N_HEADS * HEAD_DIM
KV_WIDTH = N_KV_HEADS * HEAD_DIM
CHUNK = 64
WINDOW = 128
KEYS = WINDOW + CHUNK
D_FF = 2816
CONV_W = 3
EPS = 1e-6
NEG_INF = -1e30
LOG2E = 1.4426950408889634
COL_U = 0
COL_VG = COL_U + GMLP_WIDTH
COL_Q = COL_VG + GMLP_WIDTH
COL_K = COL_Q + ATTN_WIDTH
COL_V = COL_K + KV_WIDTH
COL_GATE = COL_V + KV_WIDTH
IN_COLS = COL_GATE + 2 * D_MODEL

LANES = 128
SUBLANES = 8
HALF = LANES // 2

TILE = 512
FF_CHUNK = 256
UNIT = 256
PROMPT_SLOTS = (
    (("m", 0),), (("m", 1),), (("m", 2),), (("m", 3),),
    (("o", 0), ("o", 1), ("g", 0)), (("o", 2), ("o", 3), ("g", 1)),
    (("g", 2), ("g", 3), ("g", 4)), (("g", 5), ("g", 6), ("g", 7)),
)
FFN_TILE = 1024
FFN_EDGE_ROWS = 256
VMEM_LIMIT_BYTES = 52 * 1024 * 1024


def _rms_norm(x, g):
    return x * lax.rsqrt(jnp.mean(x * x, axis=-1, keepdims=True) + EPS) * g


def _layer_norm(x, g, b):
    mu = jnp.mean(x, axis=-1, keepdims=True)
    xc = x - mu
    return xc * lax.rsqrt(jnp.mean(xc * xc, axis=-1, keepdims=True) + EPS) * g + b


def _col(x, c):
    return x[:, c * LANES:(c + 1) * LANES]


def _swap_halves(x):
    n = x.shape[1] // LANES
    return jnp.concatenate([pltpu.roll(_col(x, c), HALF, axis=1) for c in range(n)], axis=1)


def _low_half_mask():
    return lax.broadcasted_iota(jnp.int32, (1, LANES), 1) < HALF


def _project(x_ref, lng_ref, win_ref, lvg_ref, lvb_ref, h_scr, u_scr, vn_scr, q_scr):
    h_scr[...] = _rms_norm(x_ref[...], lng_ref[...]).astype(BF16)

    def proj(c0, n):
        return jnp.dot(h_scr[...], win_ref[:, c0:c0 + n], preferred_element_type=F32)

    u_scr[...] = jax.nn.gelu(proj(COL_U, GMLP_WIDTH))
    vn = _layer_norm(jax.nn.gelu(proj(COL_VG, GMLP_WIDTH)), lvg_ref[...], lvb_ref[...])
    vn_scr[...] = vn.astype(BF16)
    q = proj(COL_Q, ATTN_WIDTH) * (HEAD_DIM ** -0.5 * LOG2E)
    qs = _swap_halves(q)
    blocks = []
    for c in range(ATTN_WIDTH // LANES):
        blocks += [_col(q, c), _col(qs, c)]
    q_scr[...] = jnp.concatenate(blocks, axis=1).astype(BF16)
    kv = proj(COL_K, 2 * KV_WIDTH)
    return kv[:, :KV_WIDTH], kv[:, KV_WIDTH:], vn


def _kv_blocks(k, v):
    lo = _low_half_mask()
    ks, vs = _swap_halves(k), _swap_halves(v)
    kb, vb = [], []
    for j in range(N_KV_HEADS):
        c = j // 2
        if j % 2 == 0:
            kb.append(jnp.where(lo, _col(k, c), 0.0))
            vb.append(jnp.where(lo, _col(v, c), _col(vs, c)))
        else:
            kb.append(jnp.where(lo, _col(ks, c), 0.0))
            vb.append(jnp.where(lo, _col(vs, c), _col(v, c)))
    return jnp.concatenate(kb, axis=1).astype(BF16), jnp.concatenate(vb, axis=1).astype(BF16)


def _scores(qc, kctx):
    out = []
    for j in range(N_KV_HEADS):
        qs = jnp.concatenate([_col(qc, GQA_GROUP * j + g) for g in range(GQA_GROUP)], axis=0)
        out.append(lax.dot_general(qs, kctx(j), (((1,), (1,)), ((), ())), preferred_element_type=F32))
    return out


def _softmax_pv(scores, vctx, sink_ref, bias_ref, first_valid_key):
    lo = _low_half_mask()
    ones = jnp.ones((KEYS, LANES), BF16)
    outs = []
    for j in range(N_KV_HEADS):
        heads = [GQA_GROUP * j + g for g in range(GQA_GROUP)]
        s = scores[j]
        es, ms = [], []
        for g, h in enumerate(heads):
            sg = s[g * CHUNK:(g + 1) * CHUNK] - bias_ref[h]
            if first_valid_key is not None:
                kj = lax.broadcasted_iota(jnp.int32, (1, KEYS), 1)
                sg = jnp.where(kj >= first_valid_key, sg, NEG_INF)
            m = jnp.maximum(jnp.max(sg, axis=-1, keepdims=True), sink_ref[h] * LOG2E)
            es.append(jnp.exp2(sg - m).astype(BF16))
            ms.append(m)
        o = jnp.dot(jnp.concatenate(es, axis=0), jnp.concatenate([vctx(j), ones], axis=1),
                    preferred_element_type=F32)
        for p in range(GQA_GROUP // 2):
            g0, g1 = 2 * p, 2 * p + 1
            o0, o1 = o[g0 * CHUNK:(g0 + 1) * CHUNK], o[g1 * CHUNK:(g1 + 1) * CHUNK]
            num = jnp.where(lo, o0[:, :LANES], o1[:, :LANES])
            den = jnp.where(lo, o0[:, LANES:], o1[:, LANES:])
            sink_term = jnp.where(lo, jnp.exp2(sink_ref[heads[g0]] * LOG2E - ms[g0]),
                                  jnp.exp2(sink_ref[heads[g1]] * LOG2E - ms[g1]))
            outs.append(num / (den + sink_term))
    return outs


def _gate_unit(k, h_scr, win_ref, g_scr):
    cols = slice(k * UNIT, (k + 1) * UNIT)
    g_scr[:, cols] = jax.nn.sigmoid(
        jnp.dot(h_scr[...], win_ref[:, COL_GATE + k * UNIT:COL_GATE + (k + 1) * UNIT], preferred_element_type=F32))


def _merge_unit(i, g_scr, a_scr, b_scr, m_scr, wpa_ref, wpb_ref):
    cols = slice(i * UNIT, (i + 1) * UNIT)
    ya = jnp.dot(a_scr[...], wpa_ref[:, cols], preferred_element_type=F32)
    yb = jnp.dot(b_scr[...], wpb_ref[:, cols], preferred_element_type=F32)
    m_scr[:, cols] = (g_scr[:, cols] * ya + g_scr[:, D_MODEL + i * UNIT:D_MODEL + (i + 1) * UNIT] * yb).astype(BF16)


def _out_unit(i, x_ref, m_scr, wo_ref, x1_ref):
    cols = slice(i * UNIT, (i + 1) * UNIT)
    x1_ref[:, cols] = x_ref[:, cols] + jnp.dot(m_scr[...], wo_ref[:, cols], preferred_element_type=F32)


def _mixer_prompt_kernel(x_ref, xprev_ref, lng_ref, win_ref, lvg_ref, lvb_ref, ws_ref, bsb_ref, sink_ref, bias_ref,
                         wpa_ref, wpb_ref, wo_ref,
                         x1_ref, knew_ref, vnew_ref,
                         h_scr, u_scr, vn_scr, q_scr, k_scr, v_scr, a_scr, b_scr, m_scr, g_scr, a_new, b_new,
                         *, n_steps, tiles_per_seq):
    t = pl.program_id(0)
    seq_tile = lax.rem(jnp.minimum(t, n_steps - 2), tiles_per_seq)

    @pl.when(t == 0)
    def _():
        a_scr[...] = jnp.zeros(a_scr.shape, BF16)
        b_scr[...] = jnp.zeros(b_scr.shape, BF16)
        g_scr[...] = jnp.zeros(g_scr.shape, F32)
        k_scr[0:WINDOW, :] = jnp.zeros((WINDOW, N_KV_HEADS * LANES), BF16)
        v_scr[0:WINDOW, :] = jnp.zeros((WINDOW, N_KV_HEADS * LANES), BF16)

    def run_unit(kind, i):
        if kind == "m":
            _merge_unit(i, g_scr, a_scr, b_scr, m_scr, wpa_ref, wpb_ref)
        elif kind == "o":
            _out_unit(i, xprev_ref, m_scr, wo_ref, x1_ref)
        else:
            _gate_unit(i, h_scr, win_ref, g_scr)

    k, v, _ = _project(x_ref, lng_ref, win_ref, lvg_ref, lvb_ref, h_scr, u_scr, vn_scr, q_scr)
    knew_ref[0] = k[TILE - WINDOW:]
    vnew_ref[0] = v[TILE - WINDOW:]

    ri = lax.broadcasted_iota(jnp.int32, (GMLP_CHUNK, GMLP_CHUNK), 0)
    ci = lax.broadcasted_iota(jnp.int32, (GMLP_CHUNK, GMLP_CHUNK), 1)
    causal = jnp.where((ci < CHUNK) | (ri >= CHUNK), 1.0, 0.0).astype(F32)
    n_blk = TILE // GMLP_CHUNK
    for g in range(GMLP_GROUPS):
        wm = (ws_ref[g] * causal).astype(BF16)
        vcat = jnp.concatenate(
            [vn_scr[n * GMLP_CHUNK:(n + 1) * GMLP_CHUNK, g * LANES:(g + 1) * LANES] for n in range(n_blk)], axis=1)
        sp = jnp.dot(wm, vcat, preferred_element_type=F32)
        for n in range(n_blk):
            rows = slice(n * GMLP_CHUNK, (n + 1) * GMLP_CHUNK)
            cols = slice(g * LANES, (g + 1) * LANES)
            a_new[rows, cols] = (u_scr[rows, cols] * (_col(sp, n) + bsb_ref[g])).astype(BF16)

    kb, vb = _kv_blocks(k, v)
    k_scr[WINDOW:, :] = kb
    v_scr[WINDOW:, :] = vb
    for c in range(TILE // CHUNK):
        r0 = c * CHUNK
        first_valid = None
        if c * CHUNK < WINDOW:
            first_valid = jnp.where(seq_tile == 0, WINDOW - c * CHUNK, 0)
        scores = _scores(q_scr[r0:r0 + CHUNK, :], lambda j: k_scr[r0:r0 + KEYS, j * LANES:(j + 1) * LANES])
        for kind, i in PROMPT_SLOTS[c]:
            run_unit(kind, i)
        outs = _softmax_pv(scores, lambda j: v_scr[r0:r0 + KEYS, j * LANES:(j + 1) * LANES],
                           sink_ref, bias_ref, first_valid)
        b_new[r0:r0 + CHUNK, :] = jnp.concatenate(outs, axis=1).astype(BF16)
    k_scr[0:WINDOW, :] = k_scr[TILE:TILE + WINDOW, :]
    v_scr[0:WINDOW, :] = v_scr[TILE:TILE + WINDOW, :]
    a_scr[...] = a_new[...]
    b_scr[...] = b_new[...]


def _mixer_sample_kernel(x_ref, lng_ref, win_ref, lvg_ref, lvb_ref, ws_ref, bsb_ref, sink_ref, bias_ref,
                         wpa_ref, wpb_ref, wo_ref, ck_ref, cv_ref,
                         x1_ref, knew_ref, vnew_ref, gv_ref,
                         h_scr, u_scr, vn_scr, q_scr, k_scr, v_scr, a_scr, b_scr, m_scr, g_scr):
    n_seq = TILE // CHUNK
    k, v, vn = _project(x_ref, lng_ref, win_ref, lvg_ref, lvb_ref, h_scr, u_scr, vn_scr, q_scr)
    gv_ref[...] = vn
    for c in range(n_seq):
        knew_ref[c] = k[c * CHUNK:(c + 1) * CHUNK]
        vnew_ref[c] = v[c * CHUNK:(c + 1) * CHUNK]

    for g in range(GMLP_GROUPS):
        wm = ws_ref[g][:CHUNK, :CHUNK].astype(BF16)
        vcat = jnp.concatenate(
            [vn_scr[c * CHUNK:(c + 1) * CHUNK, g * LANES:(g + 1) * LANES] for c in range(n_seq)], axis=1)
        sp = jnp.dot(wm, vcat, preferred_element_type=F32)
        bs = bsb_ref[g][:CHUNK]
        for c in range(n_seq):
            rows = slice(c * CHUNK, (c + 1) * CHUNK)
            cols = slice(g * LANES, (g + 1) * LANES)
            a_scr[rows, cols] = (u_scr[rows, cols] * (_col(sp, c) + bs)).astype(BF16)

    kb, vb = _kv_blocks(k, v)
    for c in range(n_seq):
        ckb, cvb = _kv_blocks(ck_ref[c], cv_ref[c])
        k_scr[c, 0:WINDOW, :] = ckb
        v_scr[c, 0:WINDOW, :] = cvb
        k_scr[c, WINDOW:, :] = kb[c * CHUNK:(c + 1) * CHUNK]
        v_scr[c, WINDOW:, :] = vb[c * CHUNK:(c + 1) * CHUNK]
    for c in range(n_seq):
        r0 = c * CHUNK
        scores = _scores(q_scr[r0:r0 + CHUNK, :], lambda j: k_scr[c, :, j * LANES:(j + 1) * LANES])
        _gate_unit(c, h_scr, win_ref, g_scr)
        outs = _softmax_pv(scores, lambda j: v_scr[c, :, j * LANES:(j + 1) * LANES], sink_ref, bias_ref, None)
        b_scr[r0:r0 + CHUNK, :] = jnp.concatenate(outs, axis=1).astype(BF16)

    assert n_seq == 2 * D_MODEL // UNIT
    for i in range(D_MODEL // UNIT):
        _merge_unit(i, g_scr, a_scr, b_scr, m_scr, wpa_ref, wpb_ref)
    for i in range(D_MODEL // UNIT):
        _out_unit(i, x_ref, m_scr, wo_ref, x1_ref)


def _up_dots(h2_scr, wup_ref, c0):
    a = jnp.dot(h2_scr[...], wup_ref[:, c0:c0 + FF_CHUNK], preferred_element_type=F32)
    val = jnp.dot(h2_scr[...], wup_ref[:, D_FF + c0:D_FF + c0 + FF_CHUNK], preferred_element_type=F32)
    return a, val


def _conv_taps(a_s, buf, w):
    n = a_s.shape[0]
    return (w[0:1] * buf[SUBLANES - 2:SUBLANES - 2 + n, :] + w[1:2] * buf[SUBLANES - 1:SUBLANES - 1 + n, :]) \
        + w[2:3] * a_s


def _ffn_kernel(*refs, sample):
    if sample:
        (x1_ref, lnf_ref, wup_ref, wconv_ref, bconv_ref, wdown_ref, lnfin_ref, cconv_ref,
         y_ref, cnew_ref, h2_scr, g_scr, a_buf) = refs
        seq_len = CHUNK
    else:
        (x1_ref, lnf_ref, wup_ref, wconv_ref, bconv_ref, wdown_ref, lnfin_ref,
         y_ref, cnew_ref, h2_scr, g_scr, a_buf, hist_scr) = refs
        seq_len = FFN_TILE

        @pl.when(pl.program_id(1) == 0)
        def _():
            hist_scr[...] = jnp.zeros(hist_scr.shape, F32)

    n_seq = FFN_TILE // seq_len
    row_blocks = [slice(r, r + FFN_EDGE_ROWS) for r in range(0, FFN_TILE, FFN_EDGE_ROWS)]
    for rows in row_blocks:
        h2_scr[rows, :] = _rms_norm(x1_ref[rows, :], lnf_ref[...]).astype(BF16)
    for j, c0 in enumerate(range(0, D_FF, FF_CHUNK)):
        cols = slice(c0, c0 + FF_CHUNK)
        if j == 0:
            parts = [_up_dots(h2_scr.at[rows, :], wup_ref, c0) for rows in row_blocks]
            a = jnp.concatenate([p[0] for p in parts], axis=0)
            val = jnp.concatenate([p[1] for p in parts], axis=0)
        else:
            a, val = _up_dots(h2_scr, wup_ref, c0)
        w = wconv_ref[:, cols]
        taps = []
        for s in range(n_seq):
            a_s = a[s * seq_len:(s + 1) * seq_len]
            buf = a_buf.at[s]
            if sample:
                buf[SUBLANES - (CONV_W - 1):SUBLANES, :] = cconv_ref[s, :, cols]
            else:
                buf[0:SUBLANES, :] = hist_scr[:, cols]
                hist_scr[:, cols] = a_s[seq_len - SUBLANES:]
            buf[SUBLANES:, :] = a_s
            cnew_ref[s, :, cols] = a_s[seq_len - (CONV_W - 1):]
            taps.append(_conv_taps(a_s, buf, w))
        conv = bconv_ref[:, cols] + (taps[0] if n_seq == 1 else jnp.concatenate(taps, axis=0))
        g_scr[:, cols] = (jax.nn.gelu(conv) * val).astype(BF16)
    for rows in row_blocks:
        x2 = x1_ref[rows, :] + jnp.dot(g_scr[rows, :], wdown_ref[...], preferred_element_type=F32)
        y_ref[rows, :] = _rms_norm(x2, lnfin_ref[...])


def _resident(shape):
    nd = len(shape)
    return pl.BlockSpec(shape, lambda *_: (0,) * nd, pipeline_mode=pl.Buffered(1))


def _alibi_bias():
    slopes = jnp.exp2(-8.0 * jnp.arange(1, N_HEADS + 1, dtype=F32) / N_HEADS)
    qi = jnp.arange(CHUNK)[:, None]
    kj = jnp.arange(KEYS)[None, :]
    dist = jnp.abs(qi + WINDOW - kj).astype(F32)
    return slopes[:, None, None] * dist[None] * LOG2E


def _mixer(x2d, n_batch, params, caches):
    lng, win, lvg, lvb, ws, bsb, sinks, bias, wpa, wpb, wo = params
    tokens = x2d.shape[0]
    sample = caches is not None
    common_in = [
        None,
        _resident(lng.shape), _resident(win.shape), _resident(lvg.shape), _resident(lvb.shape),
        _resident(ws.shape), _resident(bsb.shape),
        pl.BlockSpec(memory_space=pltpu.SMEM),
        _resident(bias.shape), _resident(wpa.shape), _resident(wpb.shape), _resident(wo.shape),
    ]
    common_scratch = [
        pltpu.VMEM((TILE, D_MODEL), BF16),
        pltpu.VMEM((TILE, GMLP_WIDTH), F32),
        pltpu.VMEM((TILE, GMLP_WIDTH), BF16),
        pltpu.VMEM((TILE, N_HEADS * LANES), BF16),
    ]
    tail_scratch = [
        pltpu.VMEM((TILE, GMLP_WIDTH), BF16),
        pltpu.VMEM((TILE, ATTN_WIDTH), BF16),
        pltpu.VMEM((TILE, D_MODEL), BF16),
        pltpu.VMEM((TILE, 2 * D_MODEL), F32),
    ]
    if not sample:
        tiles_per_seq = tokens // n_batch // TILE
        n_tiles = n_batch * tiles_per_seq
        grid = (n_tiles + 1,)
        cur = lambda t: jnp.minimum(t, n_tiles - 1)
        prev = lambda t: jnp.maximum(t - 1, 0)
        tok_cur = pl.BlockSpec((TILE, D_MODEL), lambda t: (cur(t), 0))
        tok_prev = pl.BlockSpec((TILE, D_MODEL), lambda t: (prev(t), 0))
        kv_out = pl.BlockSpec((1, WINDOW, KV_WIDTH), lambda t: (cur(t) // tiles_per_seq, 0, 0))
        in_specs = [tok_cur, tok_prev] + common_in[1:]
        out_specs = [tok_prev, kv_out, kv_out]
        out_shape = [jax.ShapeDtypeStruct((tokens, D_MODEL), F32),
                     jax.ShapeDtypeStruct((n_batch, WINDOW, KV_WIDTH), F32),
                     jax.ShapeDtypeStruct((n_batch, WINDOW, KV_WIDTH), F32)]
        kv_scratch = [pltpu.VMEM((WINDOW + TILE, N_KV_HEADS * LANES), BF16)] * 2
        tail_scratch = tail_scratch + [pltpu.VMEM((TILE, GMLP_WIDTH), BF16),
                                       pltpu.VMEM((TILE, ATTN_WIDTH), BF16)]
        body = functools.partial(_mixer_prompt_kernel, n_steps=n_tiles + 1, tiles_per_seq=tiles_per_seq)
        args, lead, name = (), (x2d, x2d), "mixer_prompt"
    else:
        n_seq = TILE // CHUNK
        grid = (tokens // TILE,)
        tok = pl.BlockSpec((TILE, D_MODEL), lambda t: (t, 0))
        cache_spec = pl.BlockSpec((n_seq, WINDOW, KV_WIDTH), lambda t: (t, 0, 0))
        kv_out = pl.BlockSpec((n_seq, CHUNK, KV_WIDTH), lambda t: (t, 0, 0))
        in_specs = [tok] + common_in[1:] + [cache_spec, cache_spec]
        out_specs = [tok, kv_out, kv_out, pl.BlockSpec((TILE, GMLP_WIDTH), lambda t: (t, 0))]
        out_shape = [jax.ShapeDtypeStruct((tokens, D_MODEL), F32),
                     jax.ShapeDtypeStruct((n_batch, CHUNK, KV_WIDTH), F32),
                     jax.ShapeDtypeStruct((n_batch, CHUNK, KV_WIDTH), F32),
                     jax.ShapeDtypeStruct((tokens, GMLP_WIDTH), F32)]
        kv_scratch = [pltpu.VMEM((n_seq, KEYS, N_KV_HEADS * LANES), BF16)] * 2
        body, args, lead, name = _mixer_sample_kernel, tuple(caches), (x2d,), "mixer_sample"
    return pl.pallas_call(
        body,
        grid=grid,
        in_specs=in_specs,
        out_specs=out_specs,
        out_shape=out_shape,
        scratch_shapes=common_scratch + kv_scratch + tail_scratch,
        compiler_params=pltpu.CompilerParams(
            dimension_semantics=("arbitrary",) * len(grid), vmem_limit_bytes=VMEM_LIMIT_BYTES),
        name=name,
    )(*lead, lng, win, lvg, lvb, ws, bsb, sinks, bias, wpa, wpb, wo, *args)


def _ffn(x2d, n_batch, params, cache_conv):
    lnf, wup, wconv, bconv, wdown, lnfin = params
    tokens = x2d.shape[0]
    sample = cache_conv is not None
    weights_in = [_resident(lnf.shape), _resident(wup.shape), _resident(wconv.shape), _resident(bconv.shape),
                  _resident(wdown.shape), _resident(lnfin.shape)]
    scratch = [pltpu.VMEM((FFN_TILE, D_MODEL), BF16),
               pltpu.VMEM((FFN_TILE, D_FF), BF16)]
    if not sample:
        n_tiles = tokens // n_batch // FFN_TILE
        grid = (n_batch, n_tiles)
        tok = pl.BlockSpec((FFN_TILE, D_MODEL), lambda b, s: (b * n_tiles + s, 0))
        in_specs = [tok] + weights_in
        conv_out = pl.BlockSpec((1, CONV_W - 1, D_FF), lambda b, s: (b, 0, 0))
        scratch += [pltpu.VMEM((1, SUBLANES + FFN_TILE, FF_CHUNK), F32),
                    pltpu.VMEM((SUBLANES, D_FF), F32)]
        args, name = (), "ffn_prompt"
    else:
        n_seq = FFN_TILE // CHUNK
        grid = (tokens // FFN_TILE,)
        tok = pl.BlockSpec((FFN_TILE, D_MODEL), lambda t: (t, 0))
        conv_out = pl.BlockSpec((n_seq, CONV_W - 1, D_FF), lambda t: (t, 0, 0))
        in_specs = [tok] + weights_in + [conv_out]
        scratch += [pltpu.VMEM((n_seq, SUBLANES + CHUNK, FF_CHUNK), F32)]
        args, name = (cache_conv,), "ffn_sample"
    return pl.pallas_call(
        functools.partial(_ffn_kernel, sample=sample),
        grid=grid,
        in_specs=in_specs,
        out_specs=[tok, conv_out],
        out_shape=[jax.ShapeDtypeStruct((tokens, D_MODEL), F32),
                   jax.ShapeDtypeStruct((n_batch, CONV_W - 1, D_FF), F32)],
        scratch_shapes=scratch,
        compiler_params=pltpu.CompilerParams(
            dimension_semantics=("arbitrary",) * len(grid), vmem_limit_bytes=VMEM_LIMIT_BYTES),
        name=name,
    )(x2d, lnf, wup, wconv, bconv, wdown, lnfin, *args)


def kernel(x_prompt, x_sample, cache_k, cache_v, cache_conv, ln_mix_g, w_in, ln_v_g, ln_v_b, w_s, b_s,
           attn_sinks, w_pa, w_pb, w_o, ln_ffn_g, w_up, w_conv, b_conv, w_down, ln_final_g):
    depth = w_in.shape[0]
    assert depth == 1, "single-layer kernel"
    n_prompt, seq, _ = x_prompt.shape
    n_sample, dec_seq, _ = x_sample.shape
    assert seq % FFN_TILE == 0 and seq % TILE == 0 and dec_seq == CHUNK
    assert (n_sample * dec_seq) % FFN_TILE == 0 and (n_sample * dec_seq) % TILE == 0
    assert cache_k.shape[2] == WINDOW

    row = lambda p: p.reshape(1, -1)
    mixer_params = (
        row(ln_mix_g[0]), w_in[0].astype(BF16), row(ln_v_g[0]), row(ln_v_b[0]),
        w_s[0], jnp.broadcast_to(b_s[0][:, :, None], (GMLP_GROUPS, GMLP_CHUNK, LANES)),
        attn_sinks[0], _alibi_bias(),
        w_pa[0].astype(BF16), w_pb[0].astype(BF16), w_o[0].astype(BF16))
    ffn_params = (row(ln_ffn_g[0]), w_up[0].astype(BF16), w_conv[0], row(b_conv[0]),
                  w_down[0].astype(BF16), row(ln_final_g))

    xp = x_prompt.reshape(n_prompt * seq, D_MODEL)
    x1p, kp, vp = _mixer(xp, n_prompt, mixer_params, None)
    yp, convp = _ffn(x1p, n_prompt, ffn_params, None)

    xs = x_sample.reshape(n_sample * dec_seq, D_MODEL)
    caches = (cache_k[0].reshape(n_sample, WINDOW, KV_WIDTH), cache_v[0].reshape(n_sample, WINDOW, KV_WIDTH))
    x1s, ks, vs, gvs = _mixer(xs, n_sample, mixer_params, caches)
    ys, convs = _ffn(x1s, n_sample, ffn_params, cache_conv[0])

    kv_shape = lambda a: a.reshape(1, a.shape[0], a.shape[1], N_KV_HEADS, HEAD_DIM)
    return (yp.reshape(n_prompt, seq, D_MODEL), ys.reshape(n_sample, dec_seq, D_MODEL),
            kv_shape(kp), kv_shape(vp), convp[None],
            kv_shape(ks), kv_shape(vs), convs[None],
            gvs.reshape(1, n_sample, dec_seq, GMLP_WIDTH))
```

```python
import functools

import jax
import jax.numpy as jnp
from jax import lax
from jax.experimental import pallas as pl
from jax.experimental.pallas import tpu as pltpu

F32 = jnp.float32
BF16 = jnp.bfloat16

D_MODEL = 1024
GMLP_WIDTH = 1024
GMLP_GROUPS = 8
GMLP_CHUNK = 128
N_HEADS = 16
N_KV_HEADS = 4
GQA_GROUP = N_HEADS // N_KV_HEADS
HEAD_DIM = 64
ATTN_WIDTH = N_HEADS * HEAD_DIM
KV_WIDTH = N_KV_HEADS * HEAD_DIM
CHUNK = 64
WINDOW = 128
KEYS = WINDOW + CHUNK
D_FF = 2816
CONV_W = 3
EPS = 1e-6
NEG_INF = -1e30
COL_U = 0
COL_VG = COL_U + GMLP_WIDTH
COL_Q = COL_VG + GMLP_WIDTH
COL_K = COL_Q + ATTN_WIDTH
COL_V = COL_K + KV_WIDTH
COL_GATE = COL_V + KV_WIDTH
IN_COLS = COL_GATE + 2 * D_MODEL

LANES = 128
SUBLANES = 8
HALF = LANES // 2

TILE = 512
FF_CHUNK = 256
UNIT = 256
PROMPT_SLOTS = (
    (("m", 0),), (("m", 1),), (("m", 2),), (("m", 3),),
    (("o", 0), ("o", 1), ("g", 0)), (("o", 2), ("o", 3), ("g", 1)),
    (("g", 2), ("g", 3), ("g", 4)), (("g", 5), ("g", 6), ("g", 7)),
)
FFN_TILE = 1024
FFN_EDGE_ROWS = 256
VMEM_LIMIT_BYTES = 52 * 1024 * 1024


def _rms_norm(x, g):
    return x * lax.rsqrt(jnp.mean(x * x, axis=-1, keepdims=True) + EPS) * g


def _layer_norm(x, g, b):
    mu = jnp.mean(x, axis=-1, keepdims=True)
    xc = x - mu
    return xc * lax.rsqrt(jnp.mean(xc * xc, axis=-1, keepdims=True) + EPS) * g + b


def _col(x, c):
    return x[:, c * LANES:(c + 1) * LANES]


def _swap_halves(x):
    n = x.shape[1] // LANES
    return jnp.concatenate([pltpu.roll(_col(x, c), HALF, axis=1) for c in range(n)], axis=1)


def _low_half_mask():
    return lax.broadcasted_iota(jnp.int32, (1, LANES), 1) < HALF


def _project(x_ref, lng_ref, win_ref, lvg_ref, lvb_ref, h_scr, u_scr, vn_scr, q_scr):
    h_scr[...] = _rms_norm(x_ref[...], lng_ref[...]).astype(BF16)

    def proj(c0, n):
        return jnp.dot(h_scr[...], win_ref[:, c0:c0 + n], preferred_element_type=F32)

    u_scr[...] = jax.nn.gelu(proj(COL_U, GMLP_WIDTH))
    vn = _layer_norm(jax.nn.gelu(proj(COL_VG, GMLP_WIDTH)), lvg_ref[...], lvb_ref[...])
    vn_scr[...] = vn.astype(BF16)
    q = proj(COL_Q, ATTN_WIDTH) * (HEAD_DIM ** -0.5)
    qs = _swap_halves(q)
    blocks = []
    for c in range(ATTN_WIDTH // LANES):
        blocks += [_col(q, c), _col(qs, c)]
    q_scr[...] = jnp.concatenate(blocks, axis=1).astype(BF16)
    kv = proj(COL_K, 2 * KV_WIDTH)
    return kv[:, :KV_WIDTH], kv[:, KV_WIDTH:], vn


def _kv_blocks(k, v):
    lo = _low_half_mask()
    ks, vs = _swap_halves(k), _swap_halves(v)
    kb, vb = [], []
    for j in range(N_KV_HEADS):
        c = j // 2
        if j % 2 == 0:
            kb.append(jnp.where(lo, _col(k, c), 0.0))
            vb.append(jnp.where(lo, _col(v, c), _col(vs, c)))
        else:
            kb.append(jnp.where(lo, _col(ks, c), 0.0))
            vb.append(jnp.where(lo, _col(vs, c), _col(v, c)))
    return jnp.concatenate(kb, axis=1).astype(BF16), jnp.concatenate(vb, axis=1).astype(BF16)


def _scores(qc, kctx):
    out = []
    for j in range(N_KV_HEADS):
        qs = jnp.concatenate([_col(qc, GQA_GROUP * j + g) for g in range(GQA_GROUP)], axis=0)
        out.append(lax.dot_general(qs, kctx(j), (((1,), (1,)), ((), ())), preferred_element_type=F32))
    return out


def _softmax_pv(scores, vctx, sink_ref, bias_ref, first_valid_key):
    lo = _low_half_mask()
    ones = jnp.ones((KEYS, LANES), BF16)
    outs = []
    for j in range(N_KV_HEADS):
        heads = [GQA_GROUP * j + g for g in range(GQA_GROUP)]
        s = scores[j]
        es, ms = [], []
        for g, h in enumerate(heads):
            sg = s[g * CHUNK:(g + 1) * CHUNK] - bias_ref[h]
            if first_valid_key is not None:
                kj = lax.broadcasted_iota(jnp.int32, (1, KEYS), 1)
                sg = jnp.where(kj >= first_valid_key, sg, NEG_INF)
            m = jnp.maximum(jnp.max(sg, axis=-1, keepdims=True), sink_ref[h])
            es.append(jnp.exp(sg - m).astype(BF16))
            ms.append(m)
        o = jnp.dot(jnp.concatenate(es, axis=0), jnp.concatenate([vctx(j), ones], axis=1),
                    preferred_element_type=F32)
        for p in range(GQA_GROUP // 2):
            g0, g1 = 2 * p, 2 * p + 1
            o0, o1 = o[g0 * CHUNK:(g0 + 1) * CHUNK], o[g1 * CHUNK:(g1 + 1) * CHUNK]
            num = jnp.where(lo, o0[:, :LANES], o1[:, :LANES])
            den = jnp.where(lo, o0[:, LANES:], o1[:, LANES:])
            sink_term = jnp.where(lo, jnp.exp(sink_ref[heads[g0]] - ms[g0]),
                                  jnp.exp(sink_ref[heads[g1]] - ms[g1]))
            outs.append(num / (den + sink_term))
    return outs


def _gate_unit(k, h_scr, win_ref, g_scr):
    cols = slice(k * UNIT, (k + 1) * UNIT)
    g_scr[:, cols] = jax.nn.sigmoid(
        jnp.dot(h_scr[...], win_ref[:, COL_GATE + k * UNIT:COL_GATE + (k + 1) * UNIT], preferred_element_type=F32))


def _merge_unit(i, g_scr, a_scr, b_scr, m_scr, wpa_ref, wpb_ref):
    cols = slice(i * UNIT, (i + 1) * UNIT)
    ya = jnp.dot(a_scr[...], wpa_ref[:, cols], preferred_element_type=F32)
    yb = jnp.dot(b_scr[...], wpb_ref[:, cols], preferred_element_type=F32)
    m_scr[:, cols] = (g_scr[:, cols] * ya + g_scr[:, D_MODEL + i * UNIT:D_MODEL + (i + 1) * UNIT] * yb).astype(BF16)


def _out_unit(i, x_ref, m_scr, wo_ref, x1_ref):
    cols = slice(i * UNIT, (i + 1) * UNIT)
    x1_ref[:, cols] = x_ref[:, cols] + jnp.dot(m_scr[...], wo_ref[:, cols], preferred_element_type=F32)


def _mixer_prompt_kernel(x_ref, xprev_ref, lng_ref, win_ref, lvg_ref, lvb_ref, ws_ref, bsb_ref, sink_ref, bias_ref,
                         wpa_ref, wpb_ref, wo_ref,
                         x1_ref, knew_ref, vnew_ref,
                         h_scr, u_scr, vn_scr, q_scr, k_scr, v_scr, a_scr, b_scr, m_scr, g_scr, a_new, b_new,
                         *, n_steps, tiles_per_seq):
    t = pl.program_id(0)
    seq_tile = lax.rem(jnp.minimum(t, n_steps - 2), tiles_per_seq)

    @pl.when(t == 0)
    def _():
        a_scr[...] = jnp.zeros(a_scr.shape, BF16)
        b_scr[...] = jnp.zeros(b_scr.shape, BF16)
        g_scr[...] = jnp.zeros(g_scr.shape, F32)
        k_scr[0:WINDOW, :] = jnp.zeros((WINDOW, N_KV_HEADS * LANES), BF16)
        v_scr[0:WINDOW, :] = jnp.zeros((WINDOW, N_KV_HEADS * LANES), BF16)

    def run_unit(kind, i):
        if kind == "m":
            _merge_unit(i, g_scr, a_scr, b_scr, m_scr, wpa_ref, wpb_ref)
        elif kind == "o":
            _out_unit(i, xprev_ref, m_scr, wo_ref, x1_ref)
        else:
            _gate_unit(i, h_scr, win_ref, g_scr)

    k, v, _ = _project(x_ref, lng_ref, win_ref, lvg_ref, lvb_ref, h_scr, u_scr, vn_scr, q_scr)
    knew_ref[0] = k[TILE - WINDOW:]
    vnew_ref[0] = v[TILE - WINDOW:]

    ri = lax.broadcasted_iota(jnp.int32, (GMLP_CHUNK, GMLP_CHUNK), 0)
    ci = lax.broadcasted_iota(jnp.int32, (GMLP_CHUNK, GMLP_CHUNK), 1)
    causal = jnp.where((ci < CHUNK) | (ri >= CHUNK), 1.0, 0.0).astype(F32)
    n_blk = TILE // GMLP_CHUNK
    for g in range(GMLP_GROUPS):
        wm = (ws_ref[g] * causal).astype(BF16)
        vcat = jnp.concatenate(
            [vn_scr[n * GMLP_CHUNK:(n + 1) * GMLP_CHUNK, g * LANES:(g + 1) * LANES] for n in range(n_blk)], axis=1)
        sp = jnp.dot(wm, vcat, preferred_element_type=F32)
        for n in range(n_blk):
            rows = slice(n * GMLP_CHUNK, (n + 1) * GMLP_CHUNK)
            cols = slice(g * LANES, (g + 1) * LANES)
            a_new[rows, cols] = (u_scr[rows, cols] * (_col(sp, n) + bsb_ref[g])).astype(BF16)

    kb, vb = _kv_blocks(k, v)
    k_scr[WINDOW:, :] = kb
    v_scr[WINDOW:, :] = vb
    for c in range(TILE // CHUNK):
        r0 = c * CHUNK
        first_valid = None
        if c * CHUNK < WINDOW:
            first_valid = jnp.where(seq_tile == 0, WINDOW - c * CHUNK, 0)
        scores = _scores(q_scr[r0:r0 + CHUNK, :], lambda j: k_scr[r0:r0 + KEYS, j * LANES:(j + 1) * LANES])
        for kind, i in PROMPT_SLOTS[c]:
            run_unit(kind, i)
        outs = _softmax_pv(scores, lambda j: v_scr[r0:r0 + KEYS, j * LANES:(j + 1) * LANES],
                           sink_ref, bias_ref, first_valid)
        b_new[r0:r0 + CHUNK, :] = jnp.concatenate(outs, axis=1).astype(BF16)
    k_scr[0:WINDOW, :] = k_scr[TILE:TILE + WINDOW, :]
    v_scr[0:WINDOW, :] = v_scr[TILE:TILE + WINDOW, :]
    a_scr[...] = a_new[...]
    b_scr[...] = b_new[...]


def _mixer_sample_kernel(x_ref, lng_ref, win_ref, lvg_ref, lvb_ref, ws_ref, bsb_ref, sink_ref, bias_ref,
                         wpa_ref, wpb_ref, wo_ref, ck_ref, cv_ref,
                         x1_ref, knew_ref, vnew_ref, gv_ref,
                         h_scr, u_scr, vn_scr, q_scr, k_scr, v_scr, a_scr, b_scr, m_scr, g_scr):
    n_seq = TILE // CHUNK
    k, v, vn = _project(x_ref, lng_ref, win_ref, lvg_ref, lvb_ref, h_scr, u_scr, vn_scr, q_scr)
    gv_ref[...] = vn
    for c in range(n_seq):
        knew_ref[c] = k[c * CHUNK:(c + 1) * CHUNK]
        vnew_ref[c] = v[c * CHUNK:(c + 1) * CHUNK]

    for g in range(GMLP_GROUPS):
        wm = ws_ref[g][:CHUNK, :CHUNK].astype(BF16)
        vcat = jnp.concatenate(
            [vn_scr[c * CHUNK:(c + 1) * CHUNK, g * LANES:(g + 1) * LANES] for c in range(n_seq)], axis=1)
        sp = jnp.dot(wm, vcat, preferred_element_type=F32)
        bs = bsb_ref[g][:CHUNK]
        for c in range(n_seq):
            rows = slice(c * CHUNK, (c + 1) * CHUNK)
            cols = slice(g * LANES, (g + 1) * LANES)
            a_scr[rows, cols] = (u_scr[rows, cols] * (_col(sp, c) + bs)).astype(BF16)

    kb, vb = _kv_blocks(k, v)
    for c in range(n_seq):
        ckb, cvb = _kv_blocks(ck_ref[c], cv_ref[c])
        k_scr[c, 0:WINDOW, :] = ckb
        v_scr[c, 0:WINDOW, :] = cvb
        k_scr[c, WINDOW:, :] = kb[c * CHUNK:(c + 1) * CHUNK]
        v_scr[c, WINDOW:, :] = vb[c * CHUNK:(c + 1) * CHUNK]
    for c in range(n_seq):
        r0 = c * CHUNK
        scores = _scores(q_scr[r0:r0 + CHUNK, :], lambda j: k_scr[c, :, j * LANES:(j + 1) * LANES])
        _gate_unit(c, h_scr, win_ref, g_scr)
        outs = _softmax_pv(scores, lambda j: v_scr[c, :, j * LANES:(j + 1) * LANES], sink_ref, bias_ref, None)
        b_scr[r0:r0 + CHUNK, :] = jnp.concatenate(outs, axis=1).astype(BF16)

    assert n_seq == 2 * D_MODEL // UNIT
    for i in range(D_MODEL // UNIT):
        _merge_unit(i, g_scr, a_scr, b_scr, m_scr, wpa_ref, wpb_ref)
    for i in range(D_MODEL // UNIT):
        _out_unit(i, x_ref, m_scr, wo_ref, x1_ref)


def _up_dots(h2_scr, wup_ref, c0):
    a = jnp.dot(h2_scr[...], wup_ref[:, c0:c0 + FF_CHUNK], preferred_element_type=F32)
    val = jnp.dot(h2_scr[...], wup_ref[:, D_FF + c0:D_FF + c0 + FF_CHUNK], preferred_element_type=F32)
    return a, val


def _conv_taps(a_s, buf, w):
    n = a_s.shape[0]
    return (w[0:1] * buf[SUBLANES - 2:SUBLANES - 2 + n, :] + w[1:2] * buf[SUBLANES - 1:SUBLANES - 1 + n, :]) \
        + w[2:3] * a_s


def _ffn_kernel(*refs, sample):
    if sample:
        (x1_ref, lnf_ref, wup_ref, wconv_ref, bconv_ref, wdown_ref, lnfin_ref, cconv_ref,
         y_ref, cnew_ref, h2_scr, g_scr, a_buf) = refs
        seq_len = CHUNK
    else:
        (x1_ref, lnf_ref, wup_ref, wconv_ref, bconv_ref, wdown_ref, lnfin_ref,
         y_ref, cnew_ref, h2_scr, g_scr, a_buf, hist_scr) = refs
        seq_len = FFN_TILE

        @pl.when(pl.program_id(1) == 0)
        def _():
            hist_scr[...] = jnp.zeros(hist_scr.shape, F32)

    n_seq = FFN_TILE // seq_len
    row_blocks = [slice(r, r + FFN_EDGE_ROWS) for r in range(0, FFN_TILE, FFN_EDGE_ROWS)]
    for rows in row_blocks:
        h2_scr[rows, :] = _rms_norm(x1_ref[rows, :], lnf_ref[...]).astype(BF16)
    for j, c0 in enumerate(range(0, D_FF, FF_CHUNK)):
        cols = slice(c0, c0 + FF_CHUNK)
        if j == 0:
            parts = [_up_dots(h2_scr.at[rows, :], wup_ref, c0) for rows in row_blocks]
            a = jnp.concatenate([p[0] for p in parts], axis=0)
            val = jnp.concatenate([p[1] for p in parts], axis=0)
        else:
            a, val = _up_dots(h2_scr, wup_ref, c0)
        w = wconv_ref[:, cols]
        taps = []
        for s in range(n_seq):
            a_s = a[s * seq_len:(s + 1) * seq_len]
            buf = a_buf.at[s]
            if sample:
                buf[SUBLANES - (CONV_W - 1):SUBLANES, :] = cconv_ref[s, :, cols]
            else:
                buf[0:SUBLANES, :] = hist_scr[:, cols]
                hist_scr[:, cols] = a_s[seq_len - SUBLANES:]
            buf[SUBLANES:, :] = a_s
            cnew_ref[s, :, cols] = a_s[seq_len - (CONV_W - 1):]
            taps.append(_conv_taps(a_s, buf, w))
        conv = bconv_ref[:, cols] + (taps[0] if n_seq == 1 else jnp.concatenate(taps, axis=0))
        g_scr[:, cols] = (jax.nn.gelu(conv) * val).astype(BF16)
    for rows in row_blocks:
        x2 = x1_ref[rows, :] + jnp.dot(g_scr[rows, :], wdown_ref[...], preferred_element_type=F32)
        y_ref[rows, :] = _rms_norm(x2, lnfin_ref[...])


def _resident(shape):
    nd = len(shape)
    return pl.BlockSpec(shape, lambda *_: (0,) * nd, pipeline_mode=pl.Buffered(1))


def _alibi_bias():
    slopes = jnp.exp2(-8.0 * jnp.arange(1, N_HEADS + 1, dtype=F32) / N_HEADS)
    qi = jnp.arange(CHUNK)[:, None]
    kj = jnp.arange(KEYS)[None, :]
    dist = jnp.abs(qi + WINDOW - kj).astype(F32)
    return slopes[:, None, None] * dist[None]


def _mixer(x2d, n_batch, params, caches):
    lng, win, lvg, lvb, ws, bsb, sinks, bias, wpa, wpb, wo = params
    tokens = x2d.shape[0]
    sample = caches is not None
    common_in = [
        None,
        _resident(lng.shape), _resident(win.shape), _resident(lvg.shape), _resident(lvb.shape),
        _resident(ws.shape), _resident(bsb.shape),
        pl.BlockSpec(memory_space=pltpu.SMEM),
        _resident(bias.shape), _resident(wpa.shape), _resident(wpb.shape), _resident(wo.shape),
    ]
    common_scratch = [
        pltpu.VMEM((TILE, D_MODEL), BF16),
        pltpu.VMEM((TILE, GMLP_WIDTH), F32),
        pltpu.VMEM((TILE, GMLP_WIDTH), BF16),
        pltpu.VMEM((TILE, N_HEADS * LANES), BF16),
    ]
    tail_scratch = [
        pltpu.VMEM((TILE, GMLP_WIDTH), BF16),
        pltpu.VMEM((TILE, ATTN_WIDTH), BF16),
        pltpu.VMEM((TILE, D_MODEL), BF16),
        pltpu.VMEM((TILE, 2 * D_MODEL), F32),
    ]
    if not sample:
        tiles_per_seq = tokens // n_batch // TILE
        n_tiles = n_batch * tiles_per_seq
        grid = (n_tiles + 1,)
        cur = lambda t: jnp.minimum(t, n_tiles - 1)
        prev = lambda t: jnp.maximum(t - 1, 0)
        tok_cur = pl.BlockSpec((TILE, D_MODEL), lambda t: (cur(t), 0))
        tok_prev = pl.BlockSpec((TILE, D_MODEL), lambda t: (prev(t), 0))
        kv_out = pl.BlockSpec((1, WINDOW, KV_WIDTH), lambda t: (cur(t) // tiles_per_seq, 0, 0))
        in_specs = [tok_cur, tok_prev] + common_in[1:]
        out_specs = [tok_prev, kv_out, kv_out]
        out_shape = [jax.ShapeDtypeStruct((tokens, D_MODEL), F32),
                     jax.ShapeDtypeStruct((n_batch, WINDOW, KV_WIDTH), F32),
                     jax.ShapeDtypeStruct((n_batch, WINDOW, KV_WIDTH), F32)]
        kv_scratch = [pltpu.VMEM((WINDOW + TILE, N_KV_HEADS * LANES), BF16)] * 2
        tail_scratch = tail_scratch + [pltpu.VMEM((TILE, GMLP_WIDTH), BF16),
                                       pltpu.VMEM((TILE, ATTN_WIDTH), BF16)]
        body = functools.partial(_mixer_prompt_kernel, n_steps=n_tiles + 1, tiles_per_seq=tiles_per_seq)
        args, lead, name = (), (x2d, x2d), "mixer_prompt"
    else:
        n_seq = TILE // CHUNK
        grid = (tokens // TILE,)
        tok = pl.BlockSpec((TILE, D_MODEL), lambda t: (t, 0))
        cache_spec = pl.BlockSpec((n_seq, WINDOW, KV_WIDTH), lambda t: (t, 0, 0))
        kv_out = pl.BlockSpec((n_seq, CHUNK, KV_WIDTH), lambda t: (t, 0, 0))
        in_specs = [tok] + common_in[1:] + [cache_spec, cache_spec]
        out_specs = [tok, kv_out, kv_out, pl.BlockSpec((TILE, GMLP_WIDTH), lambda t: (t, 0))]
        out_shape = [jax.ShapeDtypeStruct((tokens, D_MODEL), F32),
                     jax.ShapeDtypeStruct((n_batch, CHUNK, KV_WIDTH), F32),
                     jax.ShapeDtypeStruct((n_batch, CHUNK, KV_WIDTH), F32),
                     jax.ShapeDtypeStruct((tokens, GMLP_WIDTH), F32)]
        kv_scratch = [pltpu.VMEM((n_seq, KEYS, N_KV_HEADS * LANES), BF16)] * 2
        body, args, lead, name = _mixer_sample_kernel, tuple(caches), (x2d,), "mixer_sample"
    return pl.pallas_call(
        body,
        grid=grid,
        in_specs=in_specs,
        out_specs=out_specs,
        out_shape=out_shape,
        scratch_shapes=common_scratch + kv_scratch + tail_scratch,
        compiler_params=pltpu.CompilerParams(
            dimension_semantics=("arbitrary",) * len(grid), vmem_limit_bytes=VMEM_LIMIT_BYTES,
            allow_input_fusion=[False] * len(lead) + [False, True, False, False, False, False, False, False,
                                                      True, True, True] + [False] * len(args)),
        name=name,
    )(*lead, lng, win, lvg, lvb, ws, bsb, sinks, bias, wpa, wpb, wo, *args)


def _ffn(x2d, n_batch, params, cache_conv):
    lnf, wup, wconv, bconv, wdown, lnfin = params
    tokens = x2d.shape[0]
    sample = cache_conv is not None
    weights_in = [_resident(lnf.shape), _resident(wup.shape), _resident(wconv.shape), _resident(bconv.shape),
                  _resident(wdown.shape), _resident(lnfin.shape)]
    scratch = [pltpu.VMEM((FFN_TILE, D_MODEL), BF16),
               pltpu.VMEM((FFN_TILE, D_FF), BF16)]
    if not sample:
        n_tiles = tokens // n_batch // FFN_TILE
        grid = (n_batch, n_tiles)
        tok = pl.BlockSpec((FFN_TILE, D_MODEL), lambda b, s: (b * n_tiles + s, 0))
        in_specs = [tok] + weights_in
        conv_out = pl.BlockSpec((1, CONV_W - 1, D_FF), lambda b, s: (b, 0, 0))
        scratch += [pltpu.VMEM((1, SUBLANES + FFN_TILE, FF_CHUNK), F32),
                    pltpu.VMEM((SUBLANES, D_FF), F32)]
        args, name = (), "ffn_prompt"
    else:
        n_seq = FFN_TILE // CHUNK
        grid = (tokens // FFN_TILE,)
        tok = pl.BlockSpec((FFN_TILE, D_MODEL), lambda t: (t, 0))
        conv_out = pl.BlockSpec((n_seq, CONV_W - 1, D_FF), lambda t: (t, 0, 0))
        in_specs = [tok] + weights_in + [conv_out]
        scratch += [pltpu.VMEM((n_seq, SUBLANES + CHUNK, FF_CHUNK), F32)]
        args, name = (cache_conv,), "ffn_sample"
    return pl.pallas_call(
        functools.partial(_ffn_kernel, sample=sample),
        grid=grid,
        in_specs=in_specs,
        out_specs=[tok, conv_out],
        out_shape=[jax.ShapeDtypeStruct((tokens, D_MODEL), F32),
                   jax.ShapeDtypeStruct((n_batch, CONV_W - 1, D_FF), F32)],
        scratch_shapes=scratch,
        compiler_params=pltpu.CompilerParams(
            dimension_semantics=("arbitrary",) * len(grid), vmem_limit_bytes=VMEM_LIMIT_BYTES,
            allow_input_fusion=[False, False, True, False, False, True, False] + [False] * len(args)),
        name=name,
    )(x2d, lnf, wup, wconv, bconv, wdown, lnfin, *args)


def kernel(x_prompt, x_sample, cache_k, cache_v, cache_conv, ln_mix_g, w_in, ln_v_g, ln_v_b, w_s, b_s,
           attn_sinks, w_pa, w_pb, w_o, ln_ffn_g, w_up, w_conv, b_conv, w_down, ln_final_g):
    depth = w_in.shape[0]
    assert depth == 1, "single-layer kernel"
    n_prompt, seq, _ = x_prompt.shape
    n_sample, dec_seq, _ = x_sample.shape
    assert seq % FFN_TILE == 0 and seq % TILE == 0 and dec_seq == CHUNK
    assert (n_sample * dec_seq) % FFN_TILE == 0 and (n_sample * dec_seq) % TILE == 0
    assert cache_k.shape[2] == WINDOW

    row = lambda p: p.reshape(1, -1)
    mixer_params = (
        row(ln_mix_g[0]), w_in[0].astype(BF16), row(ln_v_g[0]), row(ln_v_b[0]),
        w_s[0], jnp.broadcast_to(b_s[0][:, :, None], (GMLP_GROUPS, GMLP_CHUNK, LANES)),
        attn_sinks[0], _alibi_bias(),
        w_pa[0].astype(BF16), w_pb[0].astype(BF16), w_o[0].astype(BF16))
    ffn_params = (row(ln_ffn_g[0]), w_up[0].astype(BF16), w_conv[0], row(b_conv[0]),
                  w_down[0].astype(BF16), row(ln_final_g))

    xp = x_prompt.reshape(n_prompt * seq, D_MODEL)
    x1p, kp, vp = _mixer(xp, n_prompt, mixer_params, None)
    yp, convp = _ffn(x1p, n_prompt, ffn_params, None)

    xs = x_sample.reshape(n_sample * dec_seq, D_MODEL)
    caches = (cache_k[0].reshape(n_sample, WINDOW, KV_WIDTH), cache_v[0].reshape(n_sample, WINDOW, KV_WIDTH))
    x1s, ks, vs, gvs = _mixer(xs, n_sample, mixer_params, caches)
    ys, convs = _ffn(x1s, n_sample, ffn_params, cache_conv[0])

    kv_shape = lambda a: a.reshape(1, a.shape[0], a.shape[1], N_KV_HEADS, HEAD_DIM)
    return (yp.reshape(n_prompt, seq, D_MODEL), ys.reshape(n_sample, dec_seq, D_MODEL),
            kv_shape(kp), kv_shape(vp), convp[None],
            kv_shape(ks), kv_shape(vs), convs[None],
            gvs.reshape(1, n_sample, dec_seq, GMLP_WIDTH))
```
